```python
import jax, jax.numpy as jnp
from jax import lax
import numpy as np

D_MODEL = 4096
BATCH = 2
SEQ = 4096
DEPTH = 4

CHUNK = 64
Q_BLOCK = 128
D_MIX = D_MODEL
D_CONV = D_MIX // 2
D_ATT = D_MIX - D_CONV
HEAD_DIM = 128
N_ATT_HEADS = D_ATT // HEAD_DIM
CONV_WIDTH = 3
N_IN = 3 * D_CONV + 3 * D_ATT + N_ATT_HEADS
N_GROUPS = 4
EXPERTS_PER_GROUP = 8
N_EXPERTS = N_GROUPS * EXPERTS_PER_GROUP
TOP_K = 2
D_EXPERT = (3 * D_MODEL) // N_EXPERTS
COND_RANK = D_MODEL // 8
ALPHA = (2 * DEPTH) ** 0.25
BETA = (8 * DEPTH) ** -0.25
LN_EPS = 1e-5

kernel_name = "hymba_fox_shortconv_hmoe_deepnorm_adaln"


def layer_norm(x, g, b):
    xf = x.astype(jnp.float32)
    mu = jnp.mean(xf, axis=-1, keepdims=True)
    var = jnp.mean(jnp.square(xf - mu), axis=-1, keepdims=True)
    return ((xf - mu) * lax.rsqrt(var + LN_EPS) * g + b).astype(x.dtype)


def rms_norm(x, g):
    xf = x.astype(jnp.float32)
    return (xf * lax.rsqrt(jnp.mean(jnp.square(xf), axis=-1, keepdims=True) + LN_EPS) * g).astype(x.dtype)


def forgetting_attention(q, k, v, log_f):
    seq = q.shape[2]
    cum = jnp.cumsum(log_f, axis=-1)
    scale = HEAD_DIM ** -0.5
    outs = []
    for i in range(seq // Q_BLOCK):
        qs, qe = i * Q_BLOCK, (i + 1) * Q_BLOCK
        s = jnp.einsum('bhqd,bhkd->bhqk', q[:, :, qs:qe], k[:, :, :qe],
                       preferred_element_type=jnp.float32) * scale
        s = s + cum[:, :, qs:qe, None] - cum[:, :, None, :qe]
        q_pos = jnp.arange(qs, qe)[:, None]
        k_pos = jnp.arange(qe)[None, :]
        s = jnp.where(k_pos <= q_pos, s, -jnp.inf)
        p = jax.nn.softmax(s, axis=-1)
        outs.append(jnp.einsum('bhqk,bhkd->bhqd', p.astype(v.dtype), v[:, :, :qe]))
    return jnp.concatenate(outs, axis=2)


def hybrid_mixer(h, w_in, b_f, conv_w, g_conv, g_attn, w_out):
    bsz, seq, _ = h.shape
    proj = h @ w_in
    b_gate, c_gate, h_conv, q, k, v, f_logit = jnp.split(
        proj, [D_CONV, 2 * D_CONV, 3 * D_CONV, 3 * D_CONV + D_ATT,
               3 * D_CONV + 2 * D_ATT, 3 * D_CONV + 3 * D_ATT], axis=-1)
    u = c_gate * h_conv
    u_pad = jnp.pad(u, ((0, 0), (CONV_WIDTH - 1, 0), (0, 0)))
    conv = (u_pad[:, 0:seq] * conv_w[0] + u_pad[:, 1:seq + 1] * conv_w[1]
            + u_pad[:, 2:seq + 2] * conv_w[2])
    y_conv = b_gate * conv
    def heads(a):
        return a.reshape(bsz, seq, N_ATT_HEADS, HEAD_DIM).transpose(0, 2, 1, 3)
    log_f = jax.nn.log_sigmoid((f_logit + b_f).astype(jnp.float32)).transpose(0, 2, 1)
    o = forgetting_attention(heads(q), heads(k), heads(v), log_f)
    y_att = o.transpose(0, 2, 1, 3).reshape(bsz, seq, D_ATT)
    y = jnp.concatenate([rms_norm(y_conv, g_conv), rms_norm(y_att, g_attn)], axis=-1)
    return y @ w_out


def hierarchical_moe(h, w_rg, b_rg, w_re, b_re, w_gate, w_up, w_down):
    bsz, seq, d = h.shape
    t = h.reshape(bsz * seq, d)
    g_prob = jax.nn.softmax((t @ w_rg).astype(jnp.float32) + b_rg, axis=-1)
    g_val, g_idx = lax.top_k(g_prob, 1)
    e_logits = ((t @ w_re).astype(jnp.float32) + b_re).reshape(-1, N_GROUPS, EXPERTS_PER_GROUP)
    e_in_group = jnp.take_along_axis(e_logits, g_idx[:, :, None], axis=1)[:, 0]
    e_val, e_idx = lax.top_k(e_in_group, TOP_K)
    weights = jax.nn.softmax(e_val, axis=-1) * g_val
    ids = g_idx * EXPERTS_PER_GROUP + e_idx
    combine = jnp.sum(jax.nn.one_hot(ids, N_EXPERTS, dtype=jnp.float32) * weights[..., None], axis=1)
    a = jnp.einsum('td,edf->tef', t, w_gate)
    b = jnp.einsum('td,edf->tef', t, w_up)
    hid = jax.nn.silu(a) * b * combine[:, :, None].astype(t.dtype)
    out = jnp.einsum('tef,efd->td', hid, w_down)
    return out.reshape(bsz, seq, d)


def setup_inputs(seed: int = 0) -> dict:
    key = jax.random.key(seed)
    ks = jax.random.split(key, 24)
    f32 = jnp.float32

    def nrm(k, shape, scale):
        return jax.random.normal(k, shape, f32) * scale

    return {
        "x": nrm(ks[0], (BATCH, SEQ, D_MODEL), 1.0),
        "c": nrm(ks[1], (BATCH, D_MODEL), 1.0),
        "w_cond": nrm(ks[2], (D_MODEL, COND_RANK), D_MODEL ** -0.5),
        "w_mod": nrm(ks[3], (DEPTH, COND_RANK, 6 * D_MODEL), 0.1 * COND_RANK ** -0.5),
        "b_mod": nrm(ks[4], (DEPTH, 6 * D_MODEL), 0.02),
        "w_in": nrm(ks[5], (DEPTH, D_MODEL, N_IN), D_MODEL ** -0.5),
        "b_forget": 2.0 + nrm(ks[6], (DEPTH, N_ATT_HEADS), 0.5),
        "conv_w": nrm(ks[7], (DEPTH, CONV_WIDTH, D_CONV), CONV_WIDTH ** -0.5),
        "g_conv_out": 1.0 + nrm(ks[8], (DEPTH, D_CONV), 0.05),
        "g_attn_out": 1.0 + nrm(ks[9], (DEPTH, D_ATT), 0.05),
        "w_out": nrm(ks[10], (DEPTH, D_MIX, D_MODEL), BETA * D_MIX ** -0.5),
        "ln1_g": 1.0 + nrm(ks[11], (DEPTH, D_MODEL), 0.05),
        "ln1_b": nrm(ks[12], (DEPTH, D_MODEL), 0.02),
        "w_router_group": nrm(ks[13], (DEPTH, D_MODEL, N_GROUPS), D_MODEL ** -0.5),
        "b_router_group": nrm(ks[14], (DEPTH, N_GROUPS), 0.01),
        "w_router_expert": nrm(ks[15], (DEPTH, D_MODEL, N_EXPERTS), D_MODEL ** -0.5),
        "b_router_expert": nrm(ks[16], (DEPTH, N_EXPERTS), 0.01),
        "w_gate": nrm(ks[17], (DEPTH, N_EXPERTS, D_MODEL, D_EXPERT), D_MODEL ** -0.5),
        "w_up": nrm(ks[18], (DEPTH, N_EXPERTS, D_MODEL, D_EXPERT), D_MODEL ** -0.5),
        "w_down": nrm(ks[19], (DEPTH, N_EXPERTS, D_EXPERT, D_MODEL), BETA * D_EXPERT ** -0.5),
        "ln2_g": 1.0 + nrm(ks[20], (DEPTH, D_MODEL), 0.05),
        "ln2_b": nrm(ks[21], (DEPTH, D_MODEL), 0.02),
    }


def reference(x, c, w_cond, w_mod, b_mod, w_in, b_forget, conv_w, g_conv_out, g_attn_out, w_out,
              ln1_g, ln1_b, w_router_group, b_router_group, w_router_expert, b_router_expert,
              w_gate, w_up, w_down, ln2_g, ln2_b):
    e = jax.nn.silu(c) @ w_cond
    for l in range(DEPTH):
        mod = e @ w_mod[l] + b_mod[l]
        sh1, sc1, gt1, sh2, sc2, gt2 = jnp.split(mod, 6, axis=-1)
        h = x * (1.0 + sc1[:, None, :]) + sh1[:, None, :]
        y = hybrid_mixer(h, w_in[l], b_forget[l], conv_w[l], g_conv_out[l], g_attn_out[l], w_out[l])
        x = layer_norm(ALPHA * x + (1.0 + gt1[:, None, :]) * y, ln1_g[l], ln1_b[l])
        h = x * (1.0 + sc2[:, None, :]) + sh2[:, None, :]
        y = hierarchical_moe(h, w_router_group[l], b_router_group[l], w_router_expert[l],
                             b_router_expert[l], w_gate[l], w_up[l], w_down[l])
        x = layer_norm(ALPHA * x + (1.0 + gt2[:, None, :]) * y, ln2_g[l], ln2_b[l])
    return x
```

```python
import functools
import math

import jax
import jax.numpy as jnp
from jax import lax
from jax.experimental import pallas as pl
from jax.experimental.pallas import tpu as pltpu

F32 = jnp.float32
BF16 = jnp.bfloat16

LANES = 128
SUBLANES = 8
VMEM_BYTES = 64 * 1024 * 1024
HEAD_DIM = 128
TOP_K = 2
LN_EPS = 1e-5
LOG2E = math.log2(math.e)
NEG_INF = float("-inf")


def _tile(n, pref):
    t = min(n, pref)
    while n % t:
        t -= 1
    return t


def _params(semantics, block_bytes):
    limit = min(VMEM_BYTES - (4 << 20), 2 * block_bytes + (8 << 20))
    return pltpu.CompilerParams(dimension_semantics=semantics, vmem_limit_bytes=int(limit))


def _split2(a):
    hi = a.astype(BF16)
    lo = (a - hi.astype(F32)).astype(BF16)
    return hi, lo


def _dot(a, b):
    return jnp.dot(a, b, preferred_element_type=F32)


def _dot_f32x3(a, b):
    ah, al = _split2(a)
    bh, bl = _split2(b)
    return _dot(ah, bh) + (_dot(al, bh) + _dot(ah, bl))


def _sigmoid(x):
    return 1.0 / (1.0 + jnp.exp(-x))


def _cond_embed_kernel(c_ref, w_ref, e_ref):
    c = c_ref[...]
    e_ref[...] = _dot_f32x3(c * _sigmoid(c), w_ref[...])


def _mod_kernel(e_ref, w_ref, b_ref, o_ref):
    o_ref[...] = _dot_f32x3(e_ref[...], w_ref[...]) + b_ref[...]


def _modulation(c, w_cond, w_mod, b_mod):
    bsz, d = c.shape
    depth, rank, n_mod = w_mod.shape
    rows = SUBLANES
    c_pad = jnp.pad(c, ((0, rows - bsz), (0, 0)))
    e = pl.pallas_call(
        _cond_embed_kernel,
        out_shape=jax.ShapeDtypeStruct((rows, rank), F32),
        compiler_params=_params(None, 4 * (rows * d + d * rank + rows * rank)),
        name="cond_embed",
    )(c_pad, w_cond)
    tn = _tile(n_mod, 4096)
    mod = pl.pallas_call(
        _mod_kernel,
        grid=(depth, n_mod // tn),
        in_specs=[
            pl.BlockSpec((rows, rank), lambda l, j: (0, 0)),
            pl.BlockSpec((None, rank, tn), lambda l, j: (l, 0, j)),
            pl.BlockSpec((None, 1, tn), lambda l, j: (l, 0, j)),
        ],
        out_specs=pl.BlockSpec((None, rows, tn), lambda l, j: (l, 0, j)),
        out_shape=jax.ShapeDtypeStruct((depth, rows, n_mod), F32),
        compiler_params=_params(("arbitrary", "arbitrary"), 4 * (rank * tn + rows * tn + tn + rows * rank)),
        name="adaln_mod",
    )(e, w_mod, b_mod.reshape(depth, 1, n_mod))
    return mod[:, :bsz]


def _modulate_kernel(x_ref, sc_ref, sh_ref, h_ref):
    h_ref[...] = (x_ref[...] * (1.0 + sc_ref[...]) + sh_ref[...]).astype(BF16)


def _modulate(x2, sc, sh, seq):
    t, d = x2.shape
    tm = _tile(seq, 512)
    per_b = seq // tm
    vec = pl.BlockSpec((None, 1, d), lambda i: (i // per_b, 0, 0))
    return pl.pallas_call(
        _modulate_kernel,
        grid=(t // tm,),
        in_specs=[pl.BlockSpec((tm, d), lambda i: (i, 0)), vec, vec],
        out_specs=pl.BlockSpec((tm, d), lambda i: (i, 0)),
        out_shape=jax.ShapeDtypeStruct((t, d), BF16),
        compiler_params=_params(("arbitrary",), tm * d * 6),
        name="modulate",
    )(x2, sc, sh)


def _inproj_kernel(h_ref, w_ref, o_ref, wbf_ref):
    @pl.when(pl.program_id(1) == 0)
    def _():
        wbf_ref[...] = w_ref[...].astype(BF16)

    acc = _dot(h_ref[...], wbf_ref[...])
    for cb in range(o_ref.shape[0]):
        o_ref[cb] = acc[:, cb * LANES:(cb + 1) * LANES].astype(BF16)


def _inproj(h, w_in, layer, n_main):
    t, d = h.shape
    tm = _tile(t, 1024)
    tn = _tile(n_main, 512)
    return pl.pallas_call(
        _inproj_kernel,
        grid=(n_main // tn, t // tm),
        in_specs=[
            pl.BlockSpec((tm, d), lambda j, i: (i, 0)),
            pl.BlockSpec((None, d, tn), lambda j, i: (layer, 0, j)),
        ],
        out_specs=pl.BlockSpec((tn // LANES, tm, LANES), lambda j, i: (j, i, 0)),
        out_shape=jax.ShapeDtypeStruct((n_main // LANES, t, LANES), BF16),
        scratch_shapes=[pltpu.VMEM((d, tn), BF16)],
        compiler_params=_params(("arbitrary", "arbitrary"), tm * d * 2 + d * tn * 5 + tm * tn * 2),
        name="in_proj",
    )(h, w_in)


def _forget_kernel(h_ref, w_ref, b_ref, o_ref, carry_ref, *, n_heads):
    @pl.when(pl.program_id(1) == 0)
    def _():
        carry_ref[...] = jnp.zeros_like(carry_ref)

    h = h_ref[...]
    w_hi, w_lo = _split2(w_ref[...])
    f = _dot(h, w_hi) + _dot(h, w_lo) + b_ref[...]
    log_f = jnp.minimum(f, 0.0) - jnp.log(1.0 + jnp.exp(-jnp.abs(f)))
    bs = log_f.shape[0]
    row = lax.broadcasted_iota(jnp.int32, (bs, bs), 0)
    col = lax.broadcasted_iota(jnp.int32, (bs, bs), 1)
    tril = jnp.where(row >= col, 1.0, 0.0).astype(BF16)
    p0 = log_f.astype(BF16)
    r1 = log_f - p0.astype(F32)
    p1 = r1.astype(BF16)
    p2 = (r1 - p1.astype(F32)).astype(BF16)
    cs = _dot(tril, p0) + (_dot(tril, p1) + _dot(tril, p2)) + carry_ref[...]
    carry_ref[...] = cs[bs - 1:bs, :]
    o_ref[...] = cs.T[:n_heads, :]


def _forget_cumsum(h, w_f, b_f, bsz, seq, n_heads):
    t, d = h.shape
    bs = _tile(seq, 512)
    per_b = seq // bs
    w_pad = jnp.pad(w_f, ((0, 0), (0, LANES - n_heads)))
    b_pad = jnp.pad(b_f, (0, LANES - n_heads)).reshape(1, LANES)
    return pl.pallas_call(
        functools.partial(_forget_kernel, n_heads=n_heads),
        grid=(bsz, per_b),
        in_specs=[
            pl.BlockSpec((bs, d), lambda b, i: (b * per_b + i, 0)),
            pl.BlockSpec((d, LANES), lambda b, i: (0, 0)),
            pl.BlockSpec((1, LANES), lambda b, i: (0, 0)),
        ],
        out_specs=pl.BlockSpec((None, n_heads, bs), lambda b, i: (b, 0, i)),
        out_shape=jax.ShapeDtypeStruct((bsz, n_heads, seq), F32),
        scratch_shapes=[pltpu.VMEM((1, LANES), F32)],
        compiler_params=_params(("arbitrary", "arbitrary"), bs * d * 2 + d * LANES * 8 + bs * bs * 4),
        name="forget_cumsum",
    )(h, w_pad, b_pad)


def _conv_kernel(bg_ref, cg_ref, hc_ref, w_ref, g_ref, o_ref, ubuf_ref, *, per_b, d_conv):
    i = pl.program_id(0)
    ts = bg_ref.shape[1]
    halo = SUBLANES

    @pl.when(i % per_b == 0)
    def _():
        ubuf_ref[:, 0:halo, :] = jnp.zeros((ubuf_ref.shape[0], halo, LANES), F32)

    @pl.when(i % per_b != 0)
    def _():
        ubuf_ref[:, 0:halo, :] = ubuf_ref[:, ts:ts + halo, :]

    u = cg_ref[...].astype(F32) * hc_ref[...].astype(F32)
    ubuf_ref[:, halo:halo + ts, :] = u
    u1 = ubuf_ref[:, halo - 1:halo - 1 + ts, :]
    u2 = ubuf_ref[:, halo - 2:halo - 2 + ts, :]
    conv = u2 * w_ref[0] + u1 * w_ref[1] + u * w_ref[2]
    y = bg_ref[...].astype(F32) * conv
    ssq = jnp.sum(jnp.sum(y * y, axis=0), axis=-1, keepdims=True)
    r = lax.rsqrt(ssq * (1.0 / d_conv) + LN_EPS)
    o_ref[...] = (y * r[None] * g_ref[...]).astype(BF16)


def _gated_conv(proj, conv_w, g_conv, seq):
    _, t, _ = proj.shape
    width, d_conv = conv_w.shape
    assert width == 3, "causal depthwise convolution of width 3"
    nc = d_conv // LANES
    ts = _tile(seq, 512)
    per_b = seq // ts
    slab = lambda k: pl.BlockSpec((nc, ts, LANES), lambda i: (k, i, 0))
    return pl.pallas_call(
        functools.partial(_conv_kernel, per_b=per_b, d_conv=d_conv),
        grid=(t // ts,),
        in_specs=[
            slab(0), slab(1), slab(2),
            pl.BlockSpec((width, nc, 1, LANES), lambda i: (0, 0, 0, 0)),
            pl.BlockSpec((nc, 1, LANES), lambda i: (0, 0, 0)),
        ],
        out_specs=pl.BlockSpec((nc, ts, LANES), lambda i: (0, i, 0)),
        out_shape=jax.ShapeDtypeStruct((nc, t, LANES), BF16),
        scratch_shapes=[pltpu.VMEM((nc, ts + 2 * SUBLANES, LANES), F32)],
        compiler_params=_params(("arbitrary",), nc * ts * LANES * (2 * 4 + 4 * 6)),
        name="gated_conv",
    )(proj, proj, proj, conv_w.reshape(width, nc, 1, LANES), g_conv.reshape(nc, 1, LANES))


def _attn_kernel(q_ref, k_ref, v_ref, ck_ref, o_ref, *, blk, scale2):
    qi = pl.program_id(2)
    q = q_ref[...]

    def step(kb, carry, masked):
        m, l, acc = carry
        ks = pl.multiple_of(kb * blk, blk)
        k = k_ref[pl.ds(ks, blk), :]
        v = v_ref[pl.ds(ks, blk), :]
        s = lax.dot_general(q, k, (((1,), (1,)), ((), ())), preferred_element_type=F32)
        t = s * scale2 - ck_ref[kb] * LOG2E
        if masked:
            row = lax.broadcasted_iota(jnp.int32, t.shape, 0)
            col = lax.broadcasted_iota(jnp.int32, t.shape, 1)
            t = jnp.where(col <= row, t, NEG_INF)
        m_new = jnp.maximum(m, jnp.max(t, axis=1, keepdims=True))
        p = jnp.exp2(t - m_new)
        alpha = jnp.exp2(m - m_new)
        l = alpha * l + jnp.sum(p, axis=1, keepdims=True)
        acc = alpha * acc + _dot(p.astype(BF16), v)
        return m_new, l, acc

    init = (jnp.full((blk, 1), NEG_INF, F32), jnp.zeros((blk, 1), F32), jnp.zeros((blk, HEAD_DIM), F32))
    carry = lax.fori_loop(0, qi, lambda kb, c: step(kb, c, False), init)
    _, l, acc = step(qi, carry, True)
    o_ref[...] = (acc / l).astype(BF16)


def _attention(proj, cum, bsz, seq, q_slab, n_heads):
    n_slabs = proj.shape[0]
    blk = _tile(seq, 256)
    nq = seq // blk
    proj4 = proj.reshape(n_slabs, bsz, seq, LANES)
    ck = cum.reshape(bsz, n_heads, nq, 1, blk)
    kv = lambda off: pl.BlockSpec((None, None, seq, LANES), lambda b, h, i: (off + h, b, 0, 0))
    out = pl.pallas_call(
        functools.partial(_attn_kernel, blk=blk, scale2=HEAD_DIM ** -0.5 * LOG2E),
        grid=(bsz, n_heads, nq),
        in_specs=[
            pl.BlockSpec((None, None, blk, LANES), lambda b, h, i: (q_slab + h, b, i, 0)),
            kv(q_slab + n_heads), kv(q_slab + 2 * n_heads),
            pl.BlockSpec((None, None, nq, 1, blk), lambda b, h, i: (b, h, 0, 0, 0)),
        ],
        out_specs=pl.BlockSpec((None, None, blk, LANES), lambda b, h, i: (h, b, i, 0)),
        out_shape=jax.ShapeDtypeStruct((n_heads, bsz, seq, LANES), BF16),
        compiler_params=_params(("arbitrary", "arbitrary", "arbitrary"),
                                2 * seq * LANES * 2 + seq * 4 * SUBLANES + 4 * blk * LANES * 2),
        name="fox_attention",
    )(proj4, proj4, proj4, ck)
    return out.reshape(n_heads, bsz * seq, LANES)


def _outproj_kernel(yc_ref, oa_ref, ga_ref, w_ref, o_ref, ybuf_ref, *, d_att):
    nc, na = yc_ref.shape[0], oa_ref.shape[0]

    @pl.when(pl.program_id(1) == 0)
    def _():
        for cb in range(nc):
            ybuf_ref[:, cb * LANES:(cb + 1) * LANES] = yc_ref[cb]
        oa = oa_ref[...].astype(F32)
        ssq = jnp.sum(jnp.sum(oa * oa, axis=0), axis=-1, keepdims=True)
        r = lax.rsqrt(ssq * (1.0 / d_att) + LN_EPS)
        for cb in range(na):
            ybuf_ref[:, (nc + cb) * LANES:(nc + cb + 1) * LANES] = (oa[cb] * r * ga_ref[cb]).astype(BF16)

    o_ref[...] = _dot(ybuf_ref[...], w_ref[...].astype(BF16))


def _outproj(y_conv, o_att, g_attn, w_out, layer):
    nc, t, _ = y_conv.shape
    na = o_att.shape[0]
    d_mix, d = w_out.shape[1:]
    tm = _tile(t, 1024)
    tn = _tile(d, 512)
    return pl.pallas_call(
        functools.partial(_outproj_kernel, d_att=na * LANES),
        grid=(t // tm, d // tn),
        in_specs=[
            pl.BlockSpec((nc, tm, LANES), lambda i, j: (0, i, 0)),
            pl.BlockSpec((na, tm, LANES), lambda i, j: (0, i, 0)),
            pl.BlockSpec((na, 1, LANES), lambda i, j: (0, 0, 0)),
            pl.BlockSpec((None, d_mix, tn), lambda i, j: (layer, 0, j)),
        ],
        out_specs=pl.BlockSpec((tm, tn), lambda i, j: (i, j)),
        out_shape=jax.ShapeDtypeStruct((t, d), F32),
        scratch_shapes=[pltpu.VMEM((tm, d_mix), BF16)],
        compiler_params=_params(("arbitrary", "arbitrary"),
                                tm * d_mix * 2 + tm * d_mix * 2 + d_mix * tn * 5 + tm * tn * 4),
        name="out_proj",
    )(y_conv, o_att, g_attn.reshape(na, 1, LANES), w_out)


def _ln_kernel(*refs, alpha, with_h, with_router):
    x_ref, y_ref, gt_ref, g_ref, b_ref = refs[:5]
    refs = refs[5:]
    z = alpha * x_ref[...] + (1.0 + gt_ref[...]) * y_ref[...]
    mu = jnp.mean(z, axis=-1, keepdims=True)
    zc = z - mu
    var = jnp.mean(zc * zc, axis=-1, keepdims=True)
    xn = zc * lax.rsqrt(var + LN_EPS) * g_ref[...] + b_ref[...]
    if not with_h:
        (xo_ref,) = refs
        xo_ref[...] = xn
        return
    sc_ref, sh_ref = refs[:2]
    h = xn * (1.0 + sc_ref[...]) + sh_ref[...]
    if with_router:
        wr_ref, br_ref, xo_ref, ho_ref, lg_ref = refs[2:]
        lg_ref[...] = _dot_f32x3(h, wr_ref[...]) + br_ref[...]
    else:
        xo_ref, ho_ref = refs[2:]
    xo_ref[...] = xn
    ho_ref[...] = h.astype(BF16)


def _residual_ln(x2, y, gt, g, b, seq, alpha, nxt=None, router=None):
    t, d = x2.shape
    tm = _tile(seq, 256)
    per_b = seq // tm
    tok = pl.BlockSpec((tm, d), lambda i: (i, 0))
    vec = pl.BlockSpec((None, 1, d), lambda i: (i // per_b, 0, 0))
    row = pl.BlockSpec((1, d), lambda i: (0, 0))
    ins, in_specs = [x2, y, gt, g.reshape(1, d), b.reshape(1, d)], [tok, tok, vec, row, row]
    outs, out_specs = [jax.ShapeDtypeStruct((t, d), F32)], [tok]
    if nxt is not None:
        ins += list(nxt)
        in_specs += [vec, vec]
        outs.append(jax.ShapeDtypeStruct((t, d), BF16))
        out_specs.append(tok)
    if router is not None:
        ins += list(router)
        in_specs += [pl.BlockSpec((d, LANES), lambda i: (0, 0)), pl.BlockSpec((1, LANES), lambda i: (0, 0))]
        outs.append(jax.ShapeDtypeStruct((t, LANES), F32))
        out_specs.append(pl.BlockSpec((tm, LANES), lambda i: (i, 0)))
    res = pl.pallas_call(
        functools.partial(_ln_kernel, alpha=alpha, with_h=nxt is not None, with_router=router is not None),
        grid=(t // tm,),
        in_specs=in_specs,
        out_specs=out_specs,
        out_shape=outs,
        compiler_params=_params(("arbitrary",), tm * d * 24 + d * LANES * 8),
        name="residual_ln",
    )(*ins)
    return res


def _route_kernel(lg_ref, comb_ref, *, n_groups, per_group):
    lg = lg_ref[...]
    lane = lax.broadcasted_iota(jnp.int32, lg.shape, 1)
    n_exp = n_groups * per_group

    def first_argmax(vals, vmax):
        return jnp.min(jnp.where(vals == vmax, lane, LANES), axis=1, keepdims=True)

    gl = jnp.where(lane < n_groups, lg, NEG_INF)
    g_max = jnp.max(gl, axis=1, keepdims=True)
    g_val = 1.0 / jnp.sum(jnp.exp(gl - g_max), axis=1, keepdims=True)
    g_idx = first_argmax(gl, g_max)
    lo = n_groups + g_idx * per_group
    el = jnp.where((lane >= lo) & (lane < lo + per_group) & (lane < n_groups + n_exp), lg, NEG_INF)
    e1 = jnp.max(el, axis=1, keepdims=True)
    i1 = first_argmax(el, e1)
    el2 = jnp.where(lane == i1, NEG_INF, el)
    e2 = jnp.max(el2, axis=1, keepdims=True)
    i2 = first_argmax(el2, e2)
    d = jnp.exp(e2 - e1)
    w1 = g_val / (1.0 + d)
    w2 = g_val * d / (1.0 + d)
    comb_ref[...] = jnp.where(lane == i1, w1, 0.0) + jnp.where(lane == i2, w2, 0.0)


def _route(logits, n_groups, per_group):
    t = logits.shape[0]
    tm = _tile(t, 1024)
    blk = pl.BlockSpec((tm, LANES), lambda i: (i, 0))
    return pl.pallas_call(
        functools.partial(_route_kernel, n_groups=n_groups, per_group=per_group),
        grid=(t // tm,),
        in_specs=[blk],
        out_specs=blk,
        out_shape=jax.ShapeDtypeStruct((t, LANES), F32),
        compiler_params=_params(("arbitrary",), tm * LANES * 8 * 4),
        name="route",
    )(logits)


def _moe_dense_kernel(h_ref, comb_ref, wg_ref, wu_ref, wd_ref, o_ref, *, n_groups):
    e = pl.program_id(1)
    h = h_ref[...]
    a = _dot(h, wg_ref[...].astype(BF16))
    b = _dot(h, wu_ref[...].astype(BF16))
    comb = comb_ref[...]
    lane = lax.broadcasted_iota(jnp.int32, comb.shape, 1)
    w = jnp.sum(jnp.where(lane == n_groups + e, comb, 0.0), axis=1, keepdims=True)
    hid = a * _sigmoid(a) * b * w
    contrib = _dot(hid.astype(BF16), wd_ref[...].astype(BF16))

    @pl.when(e == 0)
    def _():
        o_ref[...] = contrib

    @pl.when(e != 0)
    def _():
        o_ref[...] += contrib


def _moe_dense(h, comb, w_gate, w_up, w_down, layer, n_groups):
    t, d = h.shape
    n_exp, _, f = w_gate.shape[1:]
    tm = _tile(t, 512)
    return pl.pallas_call(
        functools.partial(_moe_dense_kernel, n_groups=n_groups),
        grid=(t // tm, n_exp),
        in_specs=[
            pl.BlockSpec((tm, d), lambda i, e: (i, 0)),
            pl.BlockSpec((tm, LANES), lambda i, e: (i, 0)),
            pl.BlockSpec((None, None, d, f), lambda i, e: (layer, e, 0, 0)),
            pl.BlockSpec((None, None, d, f), lambda i, e: (layer, e, 0, 0)),
            pl.BlockSpec((None, None, f, d), lambda i, e: (layer, e, 0, 0)),
        ],
        out_specs=pl.BlockSpec((tm, d), lambda i, e: (i, 0)),
        out_shape=jax.ShapeDtypeStruct((t, d), F32),
        compiler_params=_params(("arbitrary", "arbitrary"), tm * d * 6 + 3 * d * f * 2 + tm * LANES * 4),
        name="moe_dense",
    )(h, comb, w_gate, w_up, w_down)


def kernel(x, c, w_cond, w_mod, b_mod, w_in, b_forget, conv_w, g_conv_out, g_attn_out, w_out, ln1_g, ln1_b, w_router_group, b_router_group, w_router_expert, b_router_expert, w_gate, w_up, w_down, ln2_g, ln2_b):
    bsz, seq, d = x.shape
    depth = w_mod.shape[0]
    d_conv = conv_w.shape[2]
    n_heads = b_forget.shape[1]
    n_main = 3 * d_conv + 3 * n_heads * HEAD_DIM
    n_groups = w_router_group.shape[2]
    n_exp = w_router_expert.shape[2]
    alpha = (2 * depth) ** 0.25
    assert w_in.shape[2] == n_main + n_heads and n_groups + n_exp <= LANES
    assert d_conv % LANES == 0 and d % LANES == 0

    mod = _modulation(c, w_cond, w_mod, b_mod).reshape(depth, bsz, 6, 1, d)
    part = lambda l, k: mod[l, :, k]
    pad = LANES - n_groups - n_exp
    w_router = jnp.pad(jnp.concatenate([w_router_group, w_router_expert], axis=2), ((0, 0), (0, 0), (0, pad)))
    b_router = jnp.pad(jnp.concatenate([b_router_group, b_router_expert], axis=1), ((0, 0), (0, pad)))
    wg_bf, wu_bf, wd_bf = w_gate.astype(BF16), w_up.astype(BF16), w_down.astype(BF16)

    x2 = x.reshape(bsz * seq, d)
    h = _modulate(x2, part(0, 1), part(0, 0), seq)
    for l in range(depth):
        proj = _inproj(h, w_in, l, n_main)
        cum = _forget_cumsum(h, w_in[l, :, n_main:], b_forget[l], bsz, seq, n_heads)
        y_conv = _gated_conv(proj, conv_w[l], g_conv_out[l], seq)
        o_att = _attention(proj, cum, bsz, seq, 3 * d_conv // LANES, n_heads)
        y = _outproj(y_conv, o_att, g_attn_out[l], w_out, l)
        x2, h, logits = _residual_ln(x2, y, part(l, 2), ln1_g[l], ln1_b[l], seq, alpha,
                                     nxt=(part(l, 4), part(l, 3)),
                                     router=(w_router[l], b_router[l].reshape(1, LANES)))
        comb = _route(logits, n_groups, n_exp // n_groups)
        y = _moe_dense(h, comb, wg_bf, wu_bf, wd_bf, l, n_groups)
        if l + 1 < depth:
            x2, h = _residual_ln(x2, y, part(l, 5), ln2_g[l], ln2_b[l], seq, alpha,
                                 nxt=(part(l + 1, 1), part(l + 1, 0)))
        else:
            (x2,) = _residual_ln(x2, y, part(l, 5), ln2_g[l], ln2_b[l], seq, alpha)
    return x2.reshape(bsz, seq, d)
```

```python
import functools
import math

import jax
import jax.numpy as jnp
from jax import lax
from jax.experimental import pallas as pl
from jax.experimental.pallas import tpu as pltpu

F32 = jnp.float32
BF16 = jnp.bfloat16
U32 = jnp.uint32
I32 = jnp.int32

LANES = 128
SUBLANES = 8
VMEM_BYTES = 64 * 1024 * 1024
HEAD_DIM = 128
TOP_K = 2
LN_EPS = 1e-5
LOG2E = math.log2(math.e)
NEG_INF = float("-inf")
HI16 = 0xFFFF0000


def _tile(n, pref):
    t = min(n, pref)
    while n % t:
        t -= 1
    return t


def _params(semantics, vmem_bytes):
    limit = min(VMEM_BYTES - (4 << 20), vmem_bytes + (8 << 20))
    return pltpu.CompilerParams(dimension_semantics=semantics, vmem_limit_bytes=int(limit))


def _split2(a):
    hi = a.astype(BF16)
    lo = (a - hi.astype(F32)).astype(BF16)
    return hi, lo


def _dot(a, b):
    return jnp.dot(a, b, preferred_element_type=F32)


def _dot_nt(a, b):
    return lax.dot_general(a, b, (((1,), (1,)), ((), ())), preferred_element_type=F32)


def _dot_f32x3(a, b):
    ah, al = _split2(a)
    bh, bl = _split2(b)
    return _dot(ah, bh) + (_dot(al, bh) + _dot(ah, bl))


def _sigmoid(x):
    return 1.0 / (1.0 + jnp.exp(-x))


def _pack_bf16_pair(a, b):
    ua = lax.bitcast_convert_type(a.astype(BF16).astype(F32), U32)
    ub = lax.bitcast_convert_type(b.astype(BF16).astype(F32), U32)
    return ua | lax.shift_right_logical(ub, jnp.uint32(16))


def _unpack_bf16_pair(w):
    a = lax.bitcast_convert_type(w & jnp.uint32(HI16), F32)
    b = lax.bitcast_convert_type(lax.shift_left(w, jnp.uint32(16)), F32)
    return a, b


def _cond_embed_kernel(c_ref, w_ref, e_ref):
    c = c_ref[...]
    e_ref[...] = _dot_f32x3(c * _sigmoid(c), w_ref[...])


def _mod_kernel(e_ref, w_ref, b_ref, o_ref):
    o_ref[...] = _dot_f32x3(e_ref[...], w_ref[...]) + b_ref[...]


def _modulation(c, w_cond, w_mod, b_mod):
    bsz, d = c.shape
    depth, rank, n_mod = w_mod.shape
    rows = SUBLANES
    c_pad = jnp.pad(c, ((0, rows - bsz), (0, 0)))
    e = pl.pallas_call(
        _cond_embed_kernel,
        out_shape=jax.ShapeDtypeStruct((rows, rank), F32),
        compiler_params=_params(None, 8 * (rows * d + d * rank + rows * rank)),
        name="cond_embed",
    )(c_pad, w_cond)
    tn = _tile(n_mod, 4096)
    mod = pl.pallas_call(
        _mod_kernel,
        grid=(depth, n_mod // tn),
        in_specs=[
            pl.BlockSpec((rows, rank), lambda l, j: (0, 0)),
            pl.BlockSpec((None, rank, tn), lambda l, j: (l, 0, j)),
            pl.BlockSpec((None, 1, tn), lambda l, j: (l, 0, j)),
        ],
        out_specs=pl.BlockSpec((None, rows, tn), lambda l, j: (l, 0, j)),
        out_shape=jax.ShapeDtypeStruct((depth, rows, n_mod), F32),
        compiler_params=_params(("arbitrary", "arbitrary"), 8 * (rank * tn + rows * tn + tn + rows * rank)),
        name="adaln_mod",
    )(e, w_mod, b_mod.reshape(depth, 1, n_mod))
    return mod[:, :bsz]


def _modulate_kernel(x_ref, sc_ref, sh_ref, h_ref):
    h_ref[...] = (x_ref[...] * (1.0 + sc_ref[...]) + sh_ref[...]).astype(BF16)


def _modulate(x2, sc, sh, seq):
    t, d = x2.shape
    tm = _tile(seq, 512)
    per_b = seq // tm
    vec = pl.BlockSpec((None, 1, d), lambda i: (i // per_b, 0, 0))
    return pl.pallas_call(
        _modulate_kernel,
        grid=(t // tm,),
        in_specs=[pl.BlockSpec((tm, d), lambda i: (i, 0)), vec, vec],
        out_specs=pl.BlockSpec((tm, d), lambda i: (i, 0)),
        out_shape=jax.ShapeDtypeStruct((t, d), BF16),
        compiler_params=_params(("arbitrary",), tm * d * 12),
        name="modulate",
    )(x2, sc, sh)


def _inproj_kernel(h_ref, w_ref, o_ref, wbf_ref, *, q_tiles, q_scale):
    @pl.when(pl.program_id(1) == 0)
    def _():
        j = pl.program_id(0)
        scale = jnp.where((j >= q_tiles[0]) & (j < q_tiles[1]), q_scale, 1.0).astype(F32)
        wbf_ref[...] = (w_ref[...] * scale).astype(BF16)

    acc = _dot_nt(h_ref[...], wbf_ref[...])
    for cb in range(o_ref.shape[0]):
        o_ref[cb] = acc[:, cb * LANES:(cb + 1) * LANES].astype(BF16)


def _inproj(h, w_in_t, layer, n_main, q_cols, q_scale):
    t, d = h.shape
    tm = _tile(t, 1024)
    tn = _tile(math.gcd(q_cols[0], q_cols[1] - q_cols[0]), 512)
    q_tiles = (q_cols[0] // tn, q_cols[1] // tn)
    return pl.pallas_call(
        functools.partial(_inproj_kernel, q_tiles=q_tiles, q_scale=q_scale),
        grid=(n_main // tn, t // tm),
        in_specs=[
            pl.BlockSpec((tm, d), lambda j, i: (i, 0)),
            pl.BlockSpec((None, tn, d), lambda j, i: (layer, j, 0)),
        ],
        out_specs=pl.BlockSpec((tn // LANES, tm, LANES), lambda j, i: (j, i, 0)),
        out_shape=jax.ShapeDtypeStruct((n_main // LANES, t, LANES), BF16),
        scratch_shapes=[pltpu.VMEM((tn, d), BF16)],
        compiler_params=_params(("arbitrary", "arbitrary"), 2 * tm * d * 2 + d * tn * 10 + 3 * tm * tn * 4),
        name="in_proj",
    )(h, w_in_t)


def _forget_kernel(h_ref, w_ref, b_ref, o_ref, carry_ref):
    @pl.when(pl.program_id(1) == 0)
    def _():
        carry_ref[...] = jnp.zeros_like(carry_ref)

    h = h_ref[...]
    w_hi, w_lo = _split2(w_ref[...])
    f = _dot_nt(w_hi, h) + _dot_nt(w_lo, h) + b_ref[...]
    log_f = jnp.minimum(f, 0.0) - jnp.log(1.0 + jnp.exp(-jnp.abs(f)))
    nh, bs = log_f.shape
    row = lax.broadcasted_iota(I32, (bs, bs), 0)
    col = lax.broadcasted_iota(I32, (bs, bs), 1)
    triu = jnp.where(row <= col, 1.0, 0.0).astype(BF16)
    p0 = log_f.astype(BF16)
    r1 = log_f - p0.astype(F32)
    p1 = r1.astype(BF16)
    p2 = (r1 - p1.astype(F32)).astype(BF16)
    cs = _dot(p0, triu) + (_dot(p1, triu) + _dot(p2, triu)) + carry_ref[...]
    carry_ref[...] = cs[:, bs - 1:bs]
    o_ref[...] = cs


def _forget_cumsum(h, w_f_t, b_f, bsz, seq):
    t, d = h.shape
    nh = w_f_t.shape[0]
    bs = _tile(seq, 512)
    per_b = seq // bs
    return pl.pallas_call(
        _forget_kernel,
        grid=(bsz, per_b),
        in_specs=[
            pl.BlockSpec((bs, d), lambda b, i: (b * per_b + i, 0)),
            pl.BlockSpec((nh, d), lambda b, i: (0, 0)),
            pl.BlockSpec((nh, 1), lambda b, i: (0, 0)),
        ],
        out_specs=pl.BlockSpec((None, nh, bs), lambda b, i: (b, 0, i)),
        out_shape=jax.ShapeDtypeStruct((bsz, nh, seq), F32),
        scratch_shapes=[pltpu.VMEM((nh, 1), F32)],
        compiler_params=_params(("arbitrary", "arbitrary"), 2 * bs * d * 2 + nh * d * 16 + bs * bs * 8),
        name="forget_cumsum",
    )(h, w_f_t, b_f.reshape(nh, 1))


def _conv_kernel(bg_ref, cg_ref, hc_ref, w_ref, g_ref, o_ref, ubuf_ref, *, per_b, d_conv):
    i = pl.program_id(0)
    ts = bg_ref.shape[1]
    halo = SUBLANES

    @pl.when(i % per_b == 0)
    def _():
        ubuf_ref[:, 0:halo, :] = jnp.zeros((ubuf_ref.shape[0], halo, LANES), F32)

    @pl.when(i % per_b != 0)
    def _():
        ubuf_ref[:, 0:halo, :] = ubuf_ref[:, ts:ts + halo, :]

    u = cg_ref[...].astype(F32) * hc_ref[...].astype(F32)
    ubuf_ref[:, halo:halo + ts, :] = u
    u1 = ubuf_ref[:, halo - 1:halo - 1 + ts, :]
    u2 = ubuf_ref[:, halo - 2:halo - 2 + ts, :]
    conv = u2 * w_ref[0] + u1 * w_ref[1] + u * w_ref[2]
    y = bg_ref[...].astype(F32) * conv
    ssq = jnp.sum(jnp.sum(y * y, axis=0), axis=-1, keepdims=True)
    r = lax.rsqrt(ssq * (1.0 / d_conv) + LN_EPS)
    o_ref[...] = (y * r[None] * g_ref[...]).astype(BF16)


def _gated_conv(proj, conv_w, g_conv, seq):
    _, t, _ = proj.shape
    width, d_conv = conv_w.shape
    assert width == 3, "causal depthwise convolution of width 3"
    nc = d_conv // LANES
    ts = _tile(seq, 512)
    per_b = seq // ts
    slab = lambda k: pl.BlockSpec((nc, ts, LANES), lambda i: (k, i, 0))
    return pl.pallas_call(
        functools.partial(_conv_kernel, per_b=per_b, d_conv=d_conv),
        grid=(t // ts,),
        in_specs=[
            slab(0), slab(1), slab(2),
            pl.BlockSpec((width, nc, 1, LANES), lambda i: (0, 0, 0, 0)),
            pl.BlockSpec((nc, 1, LANES), lambda i: (0, 0, 0)),
        ],
        out_specs=pl.BlockSpec((nc, ts, LANES), lambda i: (0, i, 0)),
        out_shape=jax.ShapeDtypeStruct((nc, t, LANES), BF16),
        scratch_shapes=[pltpu.VMEM((nc, ts + 2 * SUBLANES, LANES), F32)],
        compiler_params=_params(("arbitrary",), nc * ts * LANES * (2 * 4 * 2 + 4 * 8)),
        name="gated_conv",
    )(proj, proj, proj, conv_w.reshape(width, nc, 1, LANES), g_conv.reshape(nc, 1, LANES))


def _attn_kernel(q_ref, k_ref, v_ref, ck_ref, o_ref, m_ref, l_ref, acc_ref, *, blk):
    qi = pl.program_id(2)
    n_heads = q_ref.shape[0]
    m_ref[...] = jnp.full(m_ref.shape, NEG_INF, F32)
    l_ref[...] = jnp.zeros(l_ref.shape, F32)
    acc_ref[...] = jnp.zeros(acc_ref.shape, F32)

    def step(kb, masked):
        ks = pl.multiple_of(kb * blk, blk)
        for g in range(n_heads):
            k = k_ref[g, pl.ds(ks, blk), :]
            v = v_ref[g, pl.ds(ks, blk), :]
            t = _dot_nt(q_ref[g], k) - ck_ref[g, kb] * LOG2E
            if masked:
                row = lax.broadcasted_iota(I32, t.shape, 0)
                col = lax.broadcasted_iota(I32, t.shape, 1)
                t = jnp.where(col <= row, t, NEG_INF)
            m = m_ref[g]
            m_new = jnp.maximum(m, jnp.broadcast_to(jnp.max(t, axis=1, keepdims=True), m.shape))
            p = jnp.exp2(t - jnp.tile(m_new, (1, blk // LANES)))
            alpha = jnp.exp2(m - m_new)
            l_ref[g] = alpha * l_ref[g] + jnp.broadcast_to(jnp.sum(p, axis=1, keepdims=True), m.shape)
            acc_ref[g] = alpha * acc_ref[g] + _dot(p.astype(BF16), v)
            m_ref[g] = m_new

    def body(kb, carry):
        step(kb, False)
        return carry

    lax.fori_loop(0, qi, body, 0)
    step(qi, True)
    for g in range(n_heads):
        o_ref[g] = (acc_ref[g] / l_ref[g]).astype(BF16)


def _attention(proj, cum, bsz, seq, q_slab, n_heads):
    n_slabs = proj.shape[0]
    blk = _tile(seq, 256)
    nq = seq // blk
    grp = _tile(math.gcd(n_heads, q_slab), 8)
    proj4 = proj.reshape(n_slabs, bsz, seq, LANES)
    ck = cum.reshape(bsz, n_heads, nq, 1, blk)
    kv = lambda off: pl.BlockSpec((grp, None, seq, LANES), lambda b, h, i: (off // grp + h, b, 0, 0))
    out = pl.pallas_call(
        functools.partial(_attn_kernel, blk=blk),
        grid=(bsz, n_heads // grp, nq),
        in_specs=[
            pl.BlockSpec((grp, None, blk, LANES), lambda b, h, i: (q_slab // grp + h, b, i, 0)),
            kv(q_slab + n_heads), kv(q_slab + 2 * n_heads),
            pl.BlockSpec((None, grp, nq, 1, blk), lambda b, h, i: (b, h, 0, 0, 0)),
        ],
        out_specs=pl.BlockSpec((grp, None, blk, LANES), lambda b, h, i: (h, b, i, 0)),
        out_shape=jax.ShapeDtypeStruct((n_heads, bsz, seq, LANES), BF16),
        scratch_shapes=[
            pltpu.VMEM((grp, blk, LANES), F32),
            pltpu.VMEM((grp, blk, LANES), F32),
            pltpu.VMEM((grp, blk, HEAD_DIM), F32),
        ],
        compiler_params=_params(("arbitrary", "arbitrary", "arbitrary"),
                                grp * (4 * seq * LANES * 2 + 2 * seq * 4 * SUBLANES + 4 * blk * LANES * 2
                                       + 2 * blk * LANES * 4 + blk * HEAD_DIM * 4 + 8 * blk * blk * 4)),
        name="fox_attention",
    )(proj4, proj4, proj4, ck)
    return out.reshape(n_heads, bsz * seq, LANES)


def _outproj_kernel(yc_ref, oa_ref, ga_ref, w_ref, o_ref, ybuf_ref, *, d_att):
    nc, na = yc_ref.shape[0], oa_ref.shape[0]

    @pl.when(pl.program_id(1) == 0)
    def _():
        for cb in range(nc):
            ybuf_ref[:, cb * LANES:(cb + 1) * LANES] = yc_ref[cb]
        oa = oa_ref[...].astype(F32)
        ssq = jnp.sum(jnp.sum(oa * oa, axis=0), axis=-1, keepdims=True)
        r = lax.rsqrt(ssq * (1.0 / d_att) + LN_EPS)
        for cb in range(na):
            ybuf_ref[:, (nc + cb) * LANES:(nc + cb + 1) * LANES] = (oa[cb] * r * ga_ref[cb]).astype(BF16)

    o_ref[...] = _dot(ybuf_ref[...], w_ref[...].astype(BF16))


def _outproj(y_conv, o_att, g_attn, w_out, layer):
    nc, t, _ = y_conv.shape
    na = o_att.shape[0]
    d_mix, d = w_out.shape[1:]
    tm = _tile(t, 1024)
    tn = _tile(d, 512)
    return pl.pallas_call(
        functools.partial(_outproj_kernel, d_att=na * LANES),
        grid=(t // tm, d // tn),
        in_specs=[
            pl.BlockSpec((nc, tm, LANES), lambda i, j: (0, i, 0)),
            pl.BlockSpec((na, tm, LANES), lambda i, j: (0, i, 0)),
            pl.BlockSpec((na, 1, LANES), lambda i, j: (0, 0, 0)),
            pl.BlockSpec((None, d_mix, tn), lambda i, j: (layer, 0, j)),
        ],
        out_specs=pl.BlockSpec((tm, tn), lambda i, j: (i, j)),
        out_shape=jax.ShapeDtypeStruct((t, d), F32),
        scratch_shapes=[pltpu.VMEM((tm, d_mix), BF16)],
        compiler_params=_params(("arbitrary", "arbitrary"),
                                2 * tm * d_mix * 2 + tm * d_mix * 2 + d_mix * tn * 10 + 3 * tm * tn * 4),
        name="out_proj",
    )(y_conv, o_att, g_attn.reshape(na, 1, LANES), w_out)


def _ln_finish(z, g_ref, b_ref):
    mu = jnp.mean(z, axis=-1, keepdims=True)
    zc = z - mu
    var = jnp.mean(zc * zc, axis=-1, keepdims=True)
    return zc * lax.rsqrt(var + LN_EPS) * g_ref[...] + b_ref[...]


def _ln_mixer_kernel(x_ref, y_ref, gt_ref, g_ref, b_ref, sc_ref, sh_ref, wr_ref, br_ref,
                     xo_ref, hp_ref, lg_ref, *, alpha):
    xn = _ln_finish(alpha * x_ref[...] + (1.0 + gt_ref[...]) * y_ref[...], g_ref, b_ref)
    h = xn * (1.0 + sc_ref[...]) + sh_ref[...]
    half = h.shape[1] // 2
    xo_ref[...] = xn
    hp_ref[...] = _pack_bf16_pair(h[:, :half], h[:, half:])
    lg_ref[...] = _dot_f32x3(h, wr_ref[...]) + br_ref[...]


def _ln_moe_kernel(*refs, alpha, with_h):
    x_ref, y0_ref, y1_ref, wt_ref, gt_ref, g_ref, b_ref = refs[:7]
    wt = wt_ref[...]
    w0, w1 = wt[:, 0:1], wt[:, 1:2]
    a0, b0 = _unpack_bf16_pair(y0_ref[...])
    a1, b1 = _unpack_bf16_pair(y1_ref[...])
    y = jnp.concatenate([w0 * a0 + w1 * a1, w0 * b0 + w1 * b1], axis=1)
    xn = _ln_finish(alpha * x_ref[...] + (1.0 + gt_ref[...]) * y, g_ref, b_ref)
    if with_h:
        sc_ref, sh_ref, xo_ref, ho_ref = refs[7:]
        ho_ref[...] = (xn * (1.0 + sc_ref[...]) + sh_ref[...]).astype(BF16)
    else:
        (xo_ref,) = refs[7:]
    xo_ref[...] = xn


def _ln_specs(t, d, seq, tm):
    per_b = seq // tm
    tok = pl.BlockSpec((tm, d), lambda i: (i, 0))
    vec = pl.BlockSpec((None, 1, d), lambda i: (i // per_b, 0, 0))
    row = pl.BlockSpec((1, d), lambda i: (0, 0))
    return tok, vec, row


def _ln_after_mixer(x2, y, gt, g, b, sc, sh, w_router, b_router, seq, alpha):
    t, d = x2.shape
    tm = _tile(seq, 256)
    tok, vec, row = _ln_specs(t, d, seq, tm)
    return pl.pallas_call(
        functools.partial(_ln_mixer_kernel, alpha=alpha),
        grid=(t // tm,),
        in_specs=[tok, tok, vec, row, row, vec, vec,
                  pl.BlockSpec((d, LANES), lambda i: (0, 0)), pl.BlockSpec((1, LANES), lambda i: (0, 0))],
        out_specs=[tok, pl.BlockSpec((tm, d // 2), lambda i: (i, 0)), pl.BlockSpec((tm, LANES), lambda i: (i, 0))],
        out_shape=[jax.ShapeDtypeStruct((t, d), F32), jax.ShapeDtypeStruct((t, d // 2), U32),
                   jax.ShapeDtypeStruct((t, LANES), F32)],
        compiler_params=_params(("arbitrary",), tm * d * 44 + d * LANES * 16),
        name="ln_mixer",
    )(x2, y, gt, g.reshape(1, d), b.reshape(1, d), sc, sh, w_router, b_router)


def _ln_after_moe(x2, yk, wts, gt, g, b, seq, alpha, nxt=None):
    t, d = x2.shape
    tm = _tile(seq, 256)
    tok, vec, row = _ln_specs(t, d, seq, tm)
    yk_spec = lambda k: pl.BlockSpec((None, tm, d // 2), lambda i: (k, i, 0))
    ins = [x2, yk, yk, wts, gt, g.reshape(1, d), b.reshape(1, d)]
    in_specs = [tok, yk_spec(0), yk_spec(1), pl.BlockSpec((tm, LANES), lambda i: (i, 0)), vec, row, row]
    outs, out_specs = [jax.ShapeDtypeStruct((t, d), F32)], [tok]
    if nxt is not None:
        ins += list(nxt)
        in_specs += [vec, vec]
        outs.append(jax.ShapeDtypeStruct((t, d), BF16))
        out_specs.append(tok)
    return pl.pallas_call(
        functools.partial(_ln_moe_kernel, alpha=alpha, with_h=nxt is not None),
        grid=(t // tm,),
        in_specs=in_specs,
        out_specs=out_specs,
        out_shape=outs,
        compiler_params=_params(("arbitrary",), tm * d * 44),
        name="ln_moe",
    )(*ins)


def _route_kernel(lg_ref, meta_ref, wts_ref, cnt_ref, carry_ref, *, n_groups, per_group):
    @pl.when(pl.program_id(0) == 0)
    def _():
        carry_ref[...] = jnp.zeros_like(carry_ref)

    lg = lg_ref[...]
    tb = lg.shape[0]
    lane = lax.broadcasted_iota(I32, lg.shape, 1)
    n_exp = n_groups * per_group

    def first_argmax(vals, vmax):
        return jnp.min(jnp.where(vals == vmax, lane, LANES), axis=1, keepdims=True)

    gl = jnp.where(lane < n_groups, lg, NEG_INF)
    g_max = jnp.max(gl, axis=1, keepdims=True)
    g_val = 1.0 / jnp.sum(jnp.exp(gl - g_max), axis=1, keepdims=True)
    g_idx = first_argmax(gl, g_max)
    lo = n_groups + g_idx * per_group
    el = jnp.where((lane >= lo) & (lane < lo + per_group) & (lane < n_groups + n_exp), lg, NEG_INF)
    e1 = jnp.max(el, axis=1, keepdims=True)
    i1 = first_argmax(el, e1)
    el2 = jnp.where(lane == i1, NEG_INF, el)
    e2 = jnp.max(el2, axis=1, keepdims=True)
    i2 = first_argmax(el2, e2)
    d = jnp.exp(e2 - e1)
    w1 = g_val / (1.0 + d)
    w2 = g_val * d / (1.0 + d)
    wts_ref[...] = jnp.where(lane == 0, w1, jnp.where(lane == 1, w2, 0.0))

    sel = jnp.where((lane == i1) | (lane == i2), 1.0, 0.0)
    row = lax.broadcasted_iota(I32, (tb, tb), 0)
    col = lax.broadcasted_iota(I32, (tb, tb), 1)
    strict_lower = jnp.where(col < row, 1.0, 0.0).astype(BF16)
    rank = _dot(strict_lower, sel.astype(BF16)) + carry_ref[...]
    pos1 = jnp.sum(jnp.where(lane == i1, rank, 0.0), axis=1, keepdims=True).astype(I32)
    pos2 = jnp.sum(jnp.where(lane == i2, rank, 0.0), axis=1, keepdims=True).astype(I32)
    total = rank[tb - 1:tb, :] + sel[tb - 1:tb, :]
    carry_ref[...] = total
    cnt_ref[...] = total.astype(I32)
    meta_ref[...] = jnp.where(lane == 0, i1 - n_groups, jnp.where(lane == 1, i2 - n_groups,
                              jnp.where(lane == 2, pos1, jnp.where(lane == 3, pos2, 0))))


def _route(logits, n_groups, per_group):
    t = logits.shape[0]
    tb = _tile(t, 512)
    blk = pl.BlockSpec((tb, LANES), lambda i: (i, 0))
    one = pl.BlockSpec((1, LANES), lambda i: (0, 0))
    return pl.pallas_call(
        functools.partial(_route_kernel, n_groups=n_groups, per_group=per_group),
        grid=(t // tb,),
        in_specs=[blk],
        out_specs=[blk, blk, one],
        out_shape=[jax.ShapeDtypeStruct((t, LANES), I32), jax.ShapeDtypeStruct((t, LANES), F32),
                   jax.ShapeDtypeStruct((1, LANES), I32)],
        scratch_shapes=[pltpu.VMEM((1, LANES), F32)],
        compiler_params=_params(("arbitrary",), tb * LANES * 4 * 24 + tb * tb * 8),
        name="route",
    )(logits)


ROW_DMA_WINDOW = 256


def _row_copy(src_ref, dst_ref, sem, s, d):
    return pltpu.make_async_copy(src_ref.at[pl.ds(s, 1)], dst_ref.at[pl.ds(d, 1)], sem)


def _window_drain(src_ref, dst_ref, sem):
    pltpu.make_async_copy(src_ref.at[pl.ds(0, ROW_DMA_WINDOW)], dst_ref.at[pl.ds(0, ROW_DMA_WINDOW)], sem).wait()


def _permute_rows_kernel(src_row_ref, dst_row_ref, zero_row_ref, src_ref, dst_ref, zbuf_ref, sems,
                         *, n_rows, n_zero):
    if n_zero:
        zbuf_ref[...] = jnp.zeros(zbuf_ref.shape, zbuf_ref.dtype)
        zrows = zbuf_ref.shape[0]

        def zero_copy(e):
            r = pl.multiple_of(jnp.maximum(zero_row_ref[e], 0), SUBLANES)
            return pltpu.make_async_copy(zbuf_ref, dst_ref.at[pl.ds(r, zrows)], sems.at[2])

        def zero_start(e, c):
            @pl.when(zero_row_ref[e] >= 0)
            def _():
                zero_copy(e).start()
            return c

        def zero_wait(e, c):
            @pl.when(zero_row_ref[e] >= 0)
            def _():
                zero_copy(e).wait()
            return c

        lax.fori_loop(0, n_zero, zero_start, 0)
        lax.fori_loop(0, n_zero, zero_wait, 0)

    n_win = n_rows // ROW_DMA_WINDOW

    def window(w, c):
        slot = w % 2

        def issue(i, c2):
            r = w * ROW_DMA_WINDOW + i
            _row_copy(src_ref, dst_ref, sems.at[slot], src_row_ref[r], dst_row_ref[r]).start()
            return c2

        lax.fori_loop(0, ROW_DMA_WINDOW, issue, 0, unroll=8)

        @pl.when(w > 0)
        def _():
            _window_drain(src_ref, dst_ref, sems.at[1 - slot])
        return c

    lax.fori_loop(0, n_win, window, 0)
    _window_drain(src_ref, dst_ref, sems.at[(n_win - 1) % 2])


def _permute_rows(src, src_row, dst_row, n_dst, zero_row=None, zero_rows=0):
    n_rows = src_row.shape[0]
    width = src.shape[1]
    assert n_rows % ROW_DMA_WINDOW == 0 and src.shape[0] >= ROW_DMA_WINDOW and n_dst >= ROW_DMA_WINDOW
    n_zero = 0 if zero_row is None else zero_row.shape[0]
    if zero_row is None:
        zero_row = jnp.full((1,), -1, I32)
    zrows = max(zero_rows, SUBLANES)
    return pl.pallas_call(
        functools.partial(_permute_rows_kernel, n_rows=n_rows, n_zero=n_zero),
        grid_spec=pltpu.PrefetchScalarGridSpec(
            num_scalar_prefetch=3,
            grid=(1,),
            in_specs=[pl.BlockSpec(memory_space=pl.ANY)],
            out_specs=pl.BlockSpec(memory_space=pl.ANY),
            scratch_shapes=[pltpu.VMEM((zrows, width), src.dtype), pltpu.SemaphoreType.DMA((3,))],
        ),
        out_shape=jax.ShapeDtypeStruct((n_dst, width), src.dtype),
        compiler_params=_params(("arbitrary",), zrows * width * 4),
        name="permute_rows",
    )(src_row, dst_row, zero_row, src)


def _expert_ffn_kernel(te_ref, nt_ref, x_ref, wg_ref, wu_ref, wd_ref, o_ref):
    i = pl.program_id(0)

    @pl.when(i < nt_ref[0])
    def _():
        xa, xb = _unpack_bf16_pair(x_ref[...])
        xa, xb = xa.astype(BF16), xb.astype(BF16)
        half = xa.shape[1]
        a = _dot(xa, wg_ref[:half, :].astype(BF16)) + _dot(xb, wg_ref[half:, :].astype(BF16))
        b = _dot(xa, wu_ref[:half, :].astype(BF16)) + _dot(xb, wu_ref[half:, :].astype(BF16))
        hid = (a * _sigmoid(a) * b).astype(BF16)
        ya = _dot(hid, wd_ref[:, :half].astype(BF16))
        yb = _dot(hid, wd_ref[:, half:].astype(BF16))
        o_ref[...] = _pack_bf16_pair(ya, yb)

    @pl.when(i >= nt_ref[0])
    def _():
        o_ref[...] = jnp.zeros(o_ref.shape, o_ref.dtype)


def _expert_ffn(xs, tile_expert, n_tiles, w_gate, w_up, w_down, layer, tm):
    p, half = xs.shape
    d, f = w_gate.shape[2:]
    wspec = lambda shape: pl.BlockSpec((None, None) + shape, lambda i, te, nt: (layer, te[i], 0, 0))
    return pl.pallas_call(
        _expert_ffn_kernel,
        grid_spec=pltpu.PrefetchScalarGridSpec(
            num_scalar_prefetch=2,
            grid=(p // tm,),
            in_specs=[pl.BlockSpec((tm, half), lambda i, te, nt: (i, 0)),
                      wspec((d, f)), wspec((d, f)), wspec((f, d))],
            out_specs=pl.BlockSpec((tm, half), lambda i, te, nt: (i, 0)),
        ),
        out_shape=jax.ShapeDtypeStruct((p, half), U32),
        compiler_params=_params(("arbitrary",), 2 * 3 * d * f * 4 + 4 * tm * half * 4 + 3 * d * f * 2 + tm * d * 8),
        name="expert_ffn",
    )(tile_expert, n_tiles, xs, w_gate, w_up, w_down)


def _moe_sparse(hp, logits, w_gate, w_up, w_down, layer, n_groups, per_group):
    t, half = hp.shape
    n_exp = n_groups * per_group
    tm = 256 if t * TOP_K >= 256 * n_exp else 8
    meta, wts, cnt = _route(logits, n_groups, per_group)

    counts = cnt[0, n_groups:n_groups + n_exp]
    tiles_e = (counts + tm - 1) // tm
    tile_end = jnp.cumsum(tiles_e)
    base = (tile_end - tiles_e) * tm
    n_tiles = tile_end[-1]
    max_tiles = (t * TOP_K) // tm + n_exp
    all_tiles = jnp.arange(max_tiles, dtype=I32)
    tile_expert = jnp.searchsorted(tile_end, jnp.minimum(all_tiles, n_tiles - 1), side="right").astype(I32)
    slot = (base[meta[:, 0:TOP_K]] + meta[:, TOP_K:2 * TOP_K]).astype(I32)
    tok = jnp.arange(t, dtype=I32)
    zero_row = jnp.concatenate([jnp.where(counts % tm != 0, base + (tiles_e - 1) * tm, -1),
                                jnp.where(all_tiles >= n_tiles, all_tiles * tm, -1)]).astype(I32)

    p = max_tiles * tm
    xs = _permute_rows(hp, jnp.repeat(tok, TOP_K), slot.reshape(-1), p, zero_row, tm)
    ys = _expert_ffn(xs, tile_expert, n_tiles.reshape(1).astype(I32), w_gate, w_up, w_down, layer, tm)
    yk = _permute_rows(ys, slot.T.reshape(-1), jnp.tile(tok, TOP_K) + jnp.repeat(jnp.arange(TOP_K, dtype=I32) * t, t),
                       TOP_K * t)
    return yk.reshape(TOP_K, t, half), wts


def kernel(x, c, w_cond, w_mod, b_mod, w_in, b_forget, conv_w, g_conv_out, g_attn_out, w_out, ln1_g, ln1_b, w_router_group, b_router_group, w_router_expert, b_router_expert, w_gate, w_up, w_down, ln2_g, ln2_b):
    bsz, seq, d = x.shape
    depth = w_mod.shape[0]
    d_conv = conv_w.shape[2]
    n_heads = b_forget.shape[1]
    d_att = n_heads * HEAD_DIM
    n_main = 3 * d_conv + 3 * d_att
    n_groups = w_router_group.shape[2]
    n_exp = w_router_expert.shape[2]
    alpha = (2 * depth) ** 0.25
    assert w_in.shape[2] == n_main + n_heads and n_groups + n_exp <= LANES
    assert d_conv % LANES == 0 and (d // 2) % LANES == 0

    mod = _modulation(c, w_cond, w_mod, b_mod).reshape(depth, bsz, 6, 1, d)
    part = lambda l, k: mod[l, :, k]
    pad = LANES - n_groups - n_exp
    w_router = jnp.pad(jnp.concatenate([w_router_group, w_router_expert], axis=2), ((0, 0), (0, 0), (0, pad)))
    b_router = jnp.pad(jnp.concatenate([b_router_group, b_router_expert], axis=1), ((0, 0), (0, pad)))
    w_in_t = jnp.swapaxes(w_in, 1, 2)
    q_cols = (3 * d_conv, 3 * d_conv + d_att)
    q_scale = HEAD_DIM ** -0.5 * LOG2E

    x2 = x.reshape(bsz * seq, d)
    h = _modulate(x2, part(0, 1), part(0, 0), seq)
    for l in range(depth):
        proj = _inproj(h, w_in_t, l, n_main, q_cols, q_scale)
        cum = _forget_cumsum(h, w_in_t[l, n_main:], b_forget[l], bsz, seq)
        y_conv = _gated_conv(proj, conv_w[l], g_conv_out[l], seq)
        o_att = _attention(proj, cum, bsz, seq, 3 * d_conv // LANES, n_heads)
        y = _outproj(y_conv, o_att, g_attn_out[l], w_out, l)
        x2, hp, logits = _ln_after_mixer(x2, y, part(l, 2), ln1_g[l], ln1_b[l], part(l, 4), part(l, 3),
                                         w_router[l], b_router[l].reshape(1, LANES), seq, alpha)
        yk, wts = _moe_sparse(hp, logits, w_gate, w_up, w_down, l, n_groups, n_exp // n_groups)
        if l + 1 < depth:
            x2, h = _ln_after_moe(x2, yk, wts, part(l, 5), ln2_g[l], ln2_b[l], seq, alpha,
                                  nxt=(part(l + 1, 1), part(l + 1, 0)))
        else:
            (x2,) = _ln_after_moe(x2, yk, wts, part(l, 5), ln2_g[l], ln2_b[l], seq, alpha)
    return x2.reshape(bsz, seq, d)
```

```python
import functools
import math

import jax
import jax.numpy as jnp
from jax import lax
from jax.experimental import pallas as pl
from jax.experimental.pallas import tpu as pltpu

F32 = jnp.float32
BF16 = jnp.bfloat16
U32 = jnp.uint32
I32 = jnp.int32

LANES = 128
SUBLANES = 8
VMEM_BYTES = 64 * 1024 * 1024
HEAD_DIM = 128
TOP_K = 2
LN_EPS = 1e-5
LOG2E = math.log2(math.e)
NEG_INF = float("-inf")
HI16 = 0xFFFF0000


def _tile(n, pref):
    t = min(n, pref)
    while n % t:
        t -= 1
    return t


def _params(semantics, vmem_bytes):
    limit = min(VMEM_BYTES - (4 << 20), vmem_bytes + (8 << 20))
    return pltpu.CompilerParams(dimension_semantics=semantics, vmem_limit_bytes=int(limit))


def _split2(a):
    hi = a.astype(BF16)
    lo = (a - hi.astype(F32)).astype(BF16)
    return hi, lo


def _dot(a, b):
    return jnp.dot(a, b, preferred_element_type=F32)


def _dot_nt(a, b):
    return lax.dot_general(a, b, (((1,), (1,)), ((), ())), preferred_element_type=F32)


def _dot_f32x3(a, b):
    ah, al = _split2(a)
    bh, bl = _split2(b)
    return _dot(ah, bh) + (_dot(al, bh) + _dot(ah, bl))


def _sigmoid(x):
    return 1.0 / (1.0 + jnp.exp(-x))


def _pack_bf16_pair(a, b):
    ua = lax.bitcast_convert_type(a.astype(BF16).astype(F32), U32)
    ub = lax.bitcast_convert_type(b.astype(BF16).astype(F32), U32)
    return ua | lax.shift_right_logical(ub, jnp.uint32(16))


def _unpack_bf16_pair(w):
    a = lax.bitcast_convert_type(w & jnp.uint32(HI16), F32)
    b = lax.bitcast_convert_type(lax.shift_left(w, jnp.uint32(16)), F32)
    return a, b


def _store_slabs(ref, val):
    rows, width = val.shape
    r = width // LANES
    for c in range(r):
        ref[pl.ds(c, rows, stride=r), :] = val[:, c * LANES:(c + 1) * LANES]


def _load_slabs(ref, rows):
    r = ref.shape[0] // rows
    return jnp.concatenate([ref[pl.ds(c, rows, stride=r), :] for c in range(r)], axis=1)


def _cond_embed_kernel(c_ref, w_ref, e_ref):
    c = c_ref[...]
    e_ref[...] = _dot_f32x3(c * _sigmoid(c), w_ref[...])


def _mod_kernel(e_ref, w_ref, b_ref, o_ref):
    o_ref[...] = _dot_f32x3(e_ref[...], w_ref[...]) + b_ref[...]


def _modulation(c, w_cond, w_mod, b_mod):
    bsz, d = c.shape
    depth, rank, n_mod = w_mod.shape
    rows = SUBLANES
    c_pad = jnp.pad(c, ((0, rows - bsz), (0, 0)))
    e = pl.pallas_call(
        _cond_embed_kernel,
        out_shape=jax.ShapeDtypeStruct((rows, rank), F32),
        compiler_params=_params(None, 8 * (rows * d + d * rank + rows * rank)),
        name="cond_embed",
    )(c_pad, w_cond)
    tn = _tile(n_mod, 4096)
    mod = pl.pallas_call(
        _mod_kernel,
        grid=(depth, n_mod // tn),
        in_specs=[
            pl.BlockSpec((rows, rank), lambda l, j: (0, 0)),
            pl.BlockSpec((None, rank, tn), lambda l, j: (l, 0, j)),
            pl.BlockSpec((None, 1, tn), lambda l, j: (l, 0, j)),
        ],
        out_specs=pl.BlockSpec((None, rows, tn), lambda l, j: (l, 0, j)),
        out_shape=jax.ShapeDtypeStruct((depth, rows, n_mod), F32),
        compiler_params=_params(("arbitrary", "arbitrary"), 8 * (rank * tn + rows * tn + tn + rows * rank)),
        name="adaln_mod",
    )(e, w_mod, b_mod.reshape(depth, 1, n_mod))
    return mod[:, :bsz]


def _modulate_kernel(x_ref, sc_ref, sh_ref, h_ref):
    h_ref[...] = (x_ref[...] * (1.0 + sc_ref[...]) + sh_ref[...]).astype(BF16)


def _modulate(x2, sc, sh, seq):
    t, d = x2.shape
    tm = _tile(seq, 512)
    per_b = seq // tm
    vec = pl.BlockSpec((None, 1, d), lambda i: (i // per_b, 0, 0))
    return pl.pallas_call(
        _modulate_kernel,
        grid=(t // tm,),
        in_specs=[pl.BlockSpec((tm, d), lambda i: (i, 0)), vec, vec],
        out_specs=pl.BlockSpec((tm, d), lambda i: (i, 0)),
        out_shape=jax.ShapeDtypeStruct((t, d), BF16),
        compiler_params=_params(("arbitrary",), tm * d * 12),
        name="modulate",
    )(x2, sc, sh)


def _inproj_kernel(h_ref, w_ref, o_ref, wbf_ref, *, q_tiles, q_scale):
    @pl.when(pl.program_id(1) == 0)
    def _():
        j = pl.program_id(0)
        scale = jnp.where((j >= q_tiles[0]) & (j < q_tiles[1]), q_scale, 1.0).astype(F32)
        wbf_ref[...] = (w_ref[...] * scale).astype(BF16)

    acc = _dot_nt(h_ref[...], wbf_ref[...])
    for cb in range(o_ref.shape[0]):
        o_ref[cb] = acc[:, cb * LANES:(cb + 1) * LANES].astype(BF16)


def _inproj(h, w_in_t, layer, n_main, q_cols, q_scale):
    t, d = h.shape
    tm = _tile(t, 1024)
    tn = _tile(math.gcd(q_cols[0], q_cols[1] - q_cols[0]), 512)
    q_tiles = (q_cols[0] // tn, q_cols[1] // tn)
    return pl.pallas_call(
        functools.partial(_inproj_kernel, q_tiles=q_tiles, q_scale=q_scale),
        grid=(n_main // tn, t // tm),
        in_specs=[
            pl.BlockSpec((tm, d), lambda j, i: (i, 0)),
            pl.BlockSpec((None, tn, d), lambda j, i: (layer, j, 0)),
        ],
        out_specs=pl.BlockSpec((tn // LANES, tm, LANES), lambda j, i: (j, i, 0)),
        out_shape=jax.ShapeDtypeStruct((n_main // LANES, t, LANES), BF16),
        scratch_shapes=[pltpu.VMEM((tn, d), BF16)],
        compiler_params=_params(("arbitrary", "arbitrary"), 2 * tm * d * 2 + d * tn * 10 + 3 * tm * tn * 4),
        name="in_proj",
    )(h, w_in_t)


def _forget_kernel(h_ref, w_ref, b_ref, o_ref, carry_ref):
    @pl.when(pl.program_id(1) == 0)
    def _():
        carry_ref[...] = jnp.zeros_like(carry_ref)

    h = h_ref[...]
    w_hi, w_lo = _split2(w_ref[...])
    f = _dot_nt(w_hi, h) + _dot_nt(w_lo, h) + b_ref[...]
    log_f = jnp.minimum(f, 0.0) - jnp.log(1.0 + jnp.exp(-jnp.abs(f)))
    nh, bs = log_f.shape
    row = lax.broadcasted_iota(I32, (bs, bs), 0)
    col = lax.broadcasted_iota(I32, (bs, bs), 1)
    triu = jnp.where(row <= col, 1.0, 0.0).astype(BF16)
    p0 = log_f.astype(BF16)
    r1 = log_f - p0.astype(F32)
    p1 = r1.astype(BF16)
    p2 = (r1 - p1.astype(F32)).astype(BF16)
    cs = _dot(p0, triu) + (_dot(p1, triu) + _dot(p2, triu)) + carry_ref[...]
    carry_ref[...] = cs[:, bs - 1:bs]
    o_ref[...] = cs


def _forget_cumsum(h, w_f_t, b_f, bsz, seq):
    t, d = h.shape
    nh = w_f_t.shape[0]
    bs = _tile(seq, 512)
    per_b = seq // bs
    return pl.pallas_call(
        _forget_kernel,
        grid=(bsz, per_b),
        in_specs=[
            pl.BlockSpec((bs, d), lambda b, i: (b * per_b + i, 0)),
            pl.BlockSpec((nh, d), lambda b, i: (0, 0)),
            pl.BlockSpec((nh, 1), lambda b, i: (0, 0)),
        ],
        out_specs=pl.BlockSpec((None, nh, bs), lambda b, i: (b, 0, i)),
        out_shape=jax.ShapeDtypeStruct((bsz, nh, seq), F32),
        scratch_shapes=[pltpu.VMEM((nh, 1), F32)],
        compiler_params=_params(("arbitrary", "arbitrary"), 2 * bs * d * 2 + nh * d * 16 + bs * bs * 8),
        name="forget_cumsum",
    )(h, w_f_t, b_f.reshape(nh, 1))


def _conv_kernel(bg_ref, cg_ref, hc_ref, w_ref, g_ref, o_ref, ubuf_ref, *, per_b, d_conv):
    i = pl.program_id(0)
    ts = bg_ref.shape[1]
    halo = SUBLANES

    @pl.when(i % per_b == 0)
    def _():
        ubuf_ref[:, 0:halo, :] = jnp.zeros((ubuf_ref.shape[0], halo, LANES), F32)

    @pl.when(i % per_b != 0)
    def _():
        ubuf_ref[:, 0:halo, :] = ubuf_ref[:, ts:ts + halo, :]

    u = cg_ref[...].astype(F32) * hc_ref[...].astype(F32)
    ubuf_ref[:, halo:halo + ts, :] = u
    u1 = ubuf_ref[:, halo - 1:halo - 1 + ts, :]
    u2 = ubuf_ref[:, halo - 2:halo - 2 + ts, :]
    conv = u2 * w_ref[0] + u1 * w_ref[1] + u * w_ref[2]
    y = bg_ref[...].astype(F32) * conv
    ssq = jnp.sum(jnp.sum(y * y, axis=0), axis=-1, keepdims=True)
    r = lax.rsqrt(ssq * (1.0 / d_conv) + LN_EPS)
    o_ref[...] = (y * r[None] * g_ref[...]).astype(BF16)


def _gated_conv(proj, conv_w, g_conv, seq):
    _, t, _ = proj.shape
    width, d_conv = conv_w.shape
    assert width == 3, "causal depthwise convolution of width 3"
    nc = d_conv // LANES
    ts = _tile(seq, 512)
    per_b = seq // ts
    slab = lambda k: pl.BlockSpec((nc, ts, LANES), lambda i: (k, i, 0))
    return pl.pallas_call(
        functools.partial(_conv_kernel, per_b=per_b, d_conv=d_conv),
        grid=(t // ts,),
        in_specs=[
            slab(0), slab(1), slab(2),
            pl.BlockSpec((width, nc, 1, LANES), lambda i: (0, 0, 0, 0)),
            pl.BlockSpec((nc, 1, LANES), lambda i: (0, 0, 0)),
        ],
        out_specs=pl.BlockSpec((nc, ts, LANES), lambda i: (0, i, 0)),
        out_shape=jax.ShapeDtypeStruct((nc, t, LANES), BF16),
        scratch_shapes=[pltpu.VMEM((nc, ts + 2 * SUBLANES, LANES), F32)],
        compiler_params=_params(("arbitrary",), nc * ts * LANES * (2 * 4 * 2 + 4 * 8)),
        name="gated_conv",
    )(proj, proj, proj, conv_w.reshape(width, nc, 1, LANES), g_conv.reshape(nc, 1, LANES))


def _attn_kernel(q_ref, k_ref, v_ref, ck_ref, o_ref, m_ref, l_ref, acc_ref, *, blk):
    qi = pl.program_id(2)
    n_heads = q_ref.shape[0]
    m_ref[...] = jnp.full(m_ref.shape, NEG_INF, F32)
    l_ref[...] = jnp.zeros(l_ref.shape, F32)
    acc_ref[...] = jnp.zeros(acc_ref.shape, F32)
    ones = jnp.ones((blk, LANES), BF16)

    def step(kb, masked):
        ks = pl.multiple_of(kb * blk, blk)
        for g in range(n_heads):
            k = k_ref[g, pl.ds(ks, blk), :]
            v = v_ref[g, pl.ds(ks, blk), :]
            t = _dot_nt(q_ref[g], k) - ck_ref[g, kb] * LOG2E
            if masked:
                row = lax.broadcasted_iota(I32, t.shape, 0)
                col = lax.broadcasted_iota(I32, t.shape, 1)
                t = jnp.where(col <= row, t, NEG_INF)
            m = m_ref[g]
            m_new = jnp.maximum(m, jnp.broadcast_to(jnp.max(t, axis=1, keepdims=True), m.shape))
            p = jnp.exp2(t - jnp.tile(m_new, (1, blk // LANES)))
            alpha = jnp.exp2(m - m_new)
            pv = _dot(p.astype(BF16), jnp.concatenate([v, ones], axis=1))
            l_ref[g] = alpha * l_ref[g] + pv[:, HEAD_DIM:]
            acc_ref[g] = alpha * acc_ref[g] + pv[:, :HEAD_DIM]
            m_ref[g] = m_new

    def body(kb, carry):
        step(kb, False)
        return carry

    lax.fori_loop(0, qi, body, 0)
    step(qi, True)
    for g in range(n_heads):
        o_ref[g] = (acc_ref[g] / l_ref[g]).astype(BF16)


def _attention(proj, cum, bsz, seq, q_slab, n_heads):
    n_slabs = proj.shape[0]
    blk = _tile(seq, 256)
    nq = seq // blk
    grp = _tile(math.gcd(n_heads, q_slab), 8)
    proj4 = proj.reshape(n_slabs, bsz, seq, LANES)
    ck = cum.reshape(bsz, n_heads, nq, 1, blk)
    kv = lambda off: pl.BlockSpec((grp, None, seq, LANES), lambda b, h, i: (off // grp + h, b, 0, 0))
    out = pl.pallas_call(
        functools.partial(_attn_kernel, blk=blk),
        grid=(bsz, n_heads // grp, nq),
        in_specs=[
            pl.BlockSpec((grp, None, blk, LANES), lambda b, h, i: (q_slab // grp + h, b, i, 0)),
            kv(q_slab + n_heads), kv(q_slab + 2 * n_heads),
            pl.BlockSpec((None, grp, nq, 1, blk), lambda b, h, i: (b, h, 0, 0, 0)),
        ],
        out_specs=pl.BlockSpec((grp, None, blk, LANES), lambda b, h, i: (h, b, i, 0)),
        out_shape=jax.ShapeDtypeStruct((n_heads, bsz, seq, LANES), BF16),
        scratch_shapes=[
            pltpu.VMEM((grp, blk, LANES), F32),
            pltpu.VMEM((grp, blk, LANES), F32),
            pltpu.VMEM((grp, blk, HEAD_DIM), F32),
        ],
        compiler_params=_params(("arbitrary", "arbitrary", "arbitrary"),
                                grp * (4 * seq * LANES * 2 + 2 * seq * 4 * SUBLANES + 4 * blk * LANES * 2
                                       + 2 * blk * LANES * 4 + blk * HEAD_DIM * 4 + 8 * blk * blk * 4)),
        name="fox_attention",
    )(proj4, proj4, proj4, ck)
    return out.reshape(n_heads, bsz * seq, LANES)


def _outproj_kernel(yc_ref, oa_ref, ga_ref, w_ref, o_ref, ybuf_ref, *, d_att):
    nc, na = yc_ref.shape[0], oa_ref.shape[0]

    @pl.when(pl.program_id(1) == 0)
    def _():
        for cb in range(nc):
            ybuf_ref[:, cb * LANES:(cb + 1) * LANES] = yc_ref[cb]
        oa = oa_ref[...].astype(F32)
        ssq = jnp.sum(jnp.sum(oa * oa, axis=0), axis=-1, keepdims=True)
        r = lax.rsqrt(ssq * (1.0 / d_att) + LN_EPS)
        for cb in range(na):
            ybuf_ref[:, (nc + cb) * LANES:(nc + cb + 1) * LANES] = (oa[cb] * r * ga_ref[cb]).astype(BF16)

    o_ref[...] = _dot(ybuf_ref[...], w_ref[...].astype(BF16)).astype(BF16)


def _outproj(y_conv, o_att, g_attn, w_out, layer):
    nc, t, _ = y_conv.shape
    na = o_att.shape[0]
    d_mix, d = w_out.shape[1:]
    tm = _tile(t, 1024)
    tn = _tile(d, 512)
    return pl.pallas_call(
        functools.partial(_outproj_kernel, d_att=na * LANES),
        grid=(t // tm, d // tn),
        in_specs=[
            pl.BlockSpec((nc, tm, LANES), lambda i, j: (0, i, 0)),
            pl.BlockSpec((na, tm, LANES), lambda i, j: (0, i, 0)),
            pl.BlockSpec((na, 1, LANES), lambda i, j: (0, 0, 0)),
            pl.BlockSpec((None, d_mix, tn), lambda i, j: (layer, 0, j)),
        ],
        out_specs=pl.BlockSpec((tm, tn), lambda i, j: (i, j)),
        out_shape=jax.ShapeDtypeStruct((t, d), BF16),
        scratch_shapes=[pltpu.VMEM((tm, d_mix), BF16)],
        compiler_params=_params(("arbitrary", "arbitrary"),
                                2 * tm * d_mix * 2 + tm * d_mix * 2 + d_mix * tn * 10 + 3 * tm * tn * 4),
        name="out_proj",
    )(y_conv, o_att, g_attn.reshape(na, 1, LANES), w_out)


def _ln_finish(z, g_ref, b_ref):
    mu = jnp.mean(z, axis=-1, keepdims=True)
    zc = z - mu
    var = jnp.mean(zc * zc, axis=-1, keepdims=True)
    return zc * lax.rsqrt(var + LN_EPS) * g_ref[...] + b_ref[...]


def _ln_mixer_kernel(x_ref, y_ref, gt_ref, g_ref, b_ref, sc_ref, sh_ref, wr_ref, br_ref,
                     xo_ref, hp_ref, lg_ref, *, alpha):
    xn = _ln_finish(alpha * x_ref[...] + (1.0 + gt_ref[...]) * y_ref[...].astype(F32), g_ref, b_ref)
    h = xn * (1.0 + sc_ref[...]) + sh_ref[...]
    half = h.shape[1] // 2
    xo_ref[...] = xn
    _store_slabs(hp_ref, _pack_bf16_pair(h[:, :half], h[:, half:]))
    lg_ref[...] = _dot_f32x3(h, wr_ref[...]) + br_ref[...]


def _ln_moe_kernel(*refs, alpha, with_h):
    x_ref, y0_ref, y1_ref, wt_ref, gt_ref, g_ref, b_ref = refs[:7]
    wt = wt_ref[...]
    w0, w1 = wt[:, 0:1], wt[:, 1:2]
    tm = x_ref.shape[0]
    a0, b0 = _unpack_bf16_pair(_load_slabs(y0_ref, tm))
    a1, b1 = _unpack_bf16_pair(_load_slabs(y1_ref, tm))
    y = jnp.concatenate([w0 * a0 + w1 * a1, w0 * b0 + w1 * b1], axis=1)
    xn = _ln_finish(alpha * x_ref[...] + (1.0 + gt_ref[...]) * y, g_ref, b_ref)
    if with_h:
        sc_ref, sh_ref, xo_ref, ho_ref = refs[7:]
        ho_ref[...] = (xn * (1.0 + sc_ref[...]) + sh_ref[...]).astype(BF16)
    else:
        (xo_ref,) = refs[7:]
    xo_ref[...] = xn


def _ln_specs(t, d, seq, tm):
    per_b = seq // tm
    tok = pl.BlockSpec((tm, d), lambda i: (i, 0))
    vec = pl.BlockSpec((None, 1, d), lambda i: (i // per_b, 0, 0))
    row = pl.BlockSpec((1, d), lambda i: (0, 0))
    return tok, vec, row


def _ln_after_mixer(x2, y, gt, g, b, sc, sh, w_router, b_router, seq, alpha):
    t, d = x2.shape
    tm = _tile(seq, 256)
    tok, vec, row = _ln_specs(t, d, seq, tm)
    r = d // 2 // LANES
    return pl.pallas_call(
        functools.partial(_ln_mixer_kernel, alpha=alpha),
        grid=(t // tm,),
        in_specs=[tok, tok, vec, row, row, vec, vec,
                  pl.BlockSpec((d, LANES), lambda i: (0, 0)), pl.BlockSpec((1, LANES), lambda i: (0, 0))],
        out_specs=[tok, pl.BlockSpec((tm * r, LANES), lambda i: (i, 0)), pl.BlockSpec((tm, LANES), lambda i: (i, 0))],
        out_shape=[jax.ShapeDtypeStruct((t, d), F32), jax.ShapeDtypeStruct((t * r, LANES), U32),
                   jax.ShapeDtypeStruct((t, LANES), F32)],
        compiler_params=_params(("arbitrary",), tm * d * 44 + d * LANES * 16),
        name="ln_mixer",
    )(x2, y, gt, g.reshape(1, d), b.reshape(1, d), sc, sh, w_router, b_router)


def _ln_after_moe(x2, yk, wts, gt, g, b, seq, alpha, nxt=None):
    t, d = x2.shape
    tm = _tile(seq, 256)
    tok, vec, row = _ln_specs(t, d, seq, tm)
    r = yk.shape[1] // t
    yk_spec = lambda k: pl.BlockSpec((None, tm * r, LANES), lambda i: (k, i, 0))
    ins = [x2, yk, yk, wts, gt, g.reshape(1, d), b.reshape(1, d)]
    in_specs = [tok, yk_spec(0), yk_spec(1), pl.BlockSpec((tm, LANES), lambda i: (i, 0)), vec, row, row]
    outs, out_specs = [jax.ShapeDtypeStruct((t, d), F32)], [tok]
    if nxt is not None:
        ins += list(nxt)
        in_specs += [vec, vec]
        outs.append(jax.ShapeDtypeStruct((t, d), BF16))
        out_specs.append(tok)
    return pl.pallas_call(
        functools.partial(_ln_moe_kernel, alpha=alpha, with_h=nxt is not None),
        grid=(t // tm,),
        in_specs=in_specs,
        out_specs=out_specs,
        out_shape=outs,
        compiler_params=_params(("arbitrary",), tm * d * 44),
        name="ln_moe",
    )(*ins)


def _route_kernel(lg_ref, meta_ref, wts_ref, cnt_ref, carry_ref, *, n_groups, per_group):
    @pl.when(pl.program_id(0) == 0)
    def _():
        carry_ref[...] = jnp.zeros_like(carry_ref)

    lg = lg_ref[...]
    tb = lg.shape[0]
    lane = lax.broadcasted_iota(I32, lg.shape, 1)
    n_exp = n_groups * per_group

    def first_argmax(vals, vmax):
        return jnp.min(jnp.where(vals == vmax, lane, LANES), axis=1, keepdims=True)

    gl = jnp.where(lane < n_groups, lg, NEG_INF)
    g_max = jnp.max(gl, axis=1, keepdims=True)
    g_val = 1.0 / jnp.sum(jnp.exp(gl - g_max), axis=1, keepdims=True)
    g_idx = first_argmax(gl, g_max)
    lo = n_groups + g_idx * per_group
    el = jnp.where((lane >= lo) & (lane < lo + per_group) & (lane < n_groups + n_exp), lg, NEG_INF)
    e1 = jnp.max(el, axis=1, keepdims=True)
    i1 = first_argmax(el, e1)
    el2 = jnp.where(lane == i1, NEG_INF, el)
    e2 = jnp.max(el2, axis=1, keepdims=True)
    i2 = first_argmax(el2, e2)
    d = jnp.exp(e2 - e1)
    w1 = g_val / (1.0 + d)
    w2 = g_val * d / (1.0 + d)
    wts_ref[...] = jnp.where(lane == 0, w1, jnp.where(lane == 1, w2, 0.0))

    sel = jnp.where((lane == i1) | (lane == i2), 1.0, 0.0)
    row = lax.broadcasted_iota(I32, (tb, tb), 0)
    col = lax.broadcasted_iota(I32, (tb, tb), 1)
    strict_lower = jnp.where(col < row, 1.0, 0.0).astype(BF16)
    rank = _dot(strict_lower, sel.astype(BF16)) + carry_ref[...]
    pos1 = jnp.sum(jnp.where(lane == i1, rank, 0.0), axis=1, keepdims=True).astype(I32)
    pos2 = jnp.sum(jnp.where(lane == i2, rank, 0.0), axis=1, keepdims=True).astype(I32)
    total = rank[tb - 1:tb, :] + sel[tb - 1:tb, :]
    carry_ref[...] = total
    cnt_ref[...] = total.astype(I32)
    meta_ref[...] = jnp.where(lane == 0, i1 - n_groups, jnp.where(lane == 1, i2 - n_groups,
                              jnp.where(lane == 2, pos1, jnp.where(lane == 3, pos2, 0))))


def _route(logits, n_groups, per_group):
    t = logits.shape[0]
    tb = _tile(t, 512)
    blk = pl.BlockSpec((tb, LANES), lambda i: (i, 0))
    one = pl.BlockSpec((1, LANES), lambda i: (0, 0))
    return pl.pallas_call(
        functools.partial(_route_kernel, n_groups=n_groups, per_group=per_group),
        grid=(t // tb,),
        in_specs=[blk],
        out_specs=[blk, blk, one],
        out_shape=[jax.ShapeDtypeStruct((t, LANES), I32), jax.ShapeDtypeStruct((t, LANES), F32),
                   jax.ShapeDtypeStruct((1, LANES), I32)],
        scratch_shapes=[pltpu.VMEM((1, LANES), F32)],
        compiler_params=_params(("arbitrary",), tb * LANES * 4 * 24 + tb * tb * 8),
        name="route",
    )(logits)


ROW_DMA_WINDOW = 256


def _row_copy(src_ref, dst_ref, sem, s, d, r):
    return pltpu.make_async_copy(src_ref.at[pl.ds(pl.multiple_of(s * r, r), r)],
                                 dst_ref.at[pl.ds(pl.multiple_of(d * r, r), r)], sem)


def _window_drain(src_ref, dst_ref, sem, r):
    n = ROW_DMA_WINDOW * r
    pltpu.make_async_copy(src_ref.at[pl.ds(0, n)], dst_ref.at[pl.ds(0, n)], sem).wait()


def _permute_rows_kernel(src_row_ref, dst_row_ref, zero_row_ref, src_ref, dst_ref, zbuf_ref, sems,
                         *, n_rows, n_zero, r):
    if n_zero:
        zbuf_ref[...] = jnp.zeros(zbuf_ref.shape, zbuf_ref.dtype)
        zrows = zbuf_ref.shape[0]

        def zero_copy(e):
            start = pl.multiple_of(jnp.maximum(zero_row_ref[e], 0) * r, SUBLANES)
            return pltpu.make_async_copy(zbuf_ref, dst_ref.at[pl.ds(start, zrows)], sems.at[2])

        def zero_start(e, c):
            @pl.when(zero_row_ref[e] >= 0)
            def _():
                zero_copy(e).start()
            return c

        def zero_wait(e, c):
            @pl.when(zero_row_ref[e] >= 0)
            def _():
                zero_copy(e).wait()
            return c

        lax.fori_loop(0, n_zero, zero_start, 0)
        lax.fori_loop(0, n_zero, zero_wait, 0)

    n_win = n_rows // ROW_DMA_WINDOW

    def window(w, c):
        slot = w % 2

        def issue(i, c2):
            k = w * ROW_DMA_WINDOW + i
            _row_copy(src_ref, dst_ref, sems.at[slot], src_row_ref[k], dst_row_ref[k], r).start()
            return c2

        lax.fori_loop(0, ROW_DMA_WINDOW, issue, 0, unroll=8)

        @pl.when(w > 0)
        def _():
            _window_drain(src_ref, dst_ref, sems.at[1 - slot], r)
        return c

    lax.fori_loop(0, n_win, window, 0)
    _window_drain(src_ref, dst_ref, sems.at[(n_win - 1) % 2], r)


def _permute_rows(src, r, src_row, dst_row, n_dst, zero_row=None, zero_rows=0):
    n_rows = src_row.shape[0]
    assert n_rows % ROW_DMA_WINDOW == 0 and src.shape[0] >= ROW_DMA_WINDOW * r and n_dst >= ROW_DMA_WINDOW
    n_zero = 0 if zero_row is None else zero_row.shape[0]
    if zero_row is None:
        zero_row = jnp.full((1,), -1, I32)
    zrows = max(zero_rows * r, SUBLANES)
    return pl.pallas_call(
        functools.partial(_permute_rows_kernel, n_rows=n_rows, n_zero=n_zero, r=r),
        grid_spec=pltpu.PrefetchScalarGridSpec(
            num_scalar_prefetch=3,
            grid=(1,),
            in_specs=[pl.BlockSpec(memory_space=pl.ANY)],
            out_specs=pl.BlockSpec(memory_space=pl.ANY),
            scratch_shapes=[pltpu.VMEM((zrows, LANES), src.dtype), pltpu.SemaphoreType.DMA((3,))],
        ),
        out_shape=jax.ShapeDtypeStruct((n_dst * r, LANES), src.dtype),
        compiler_params=_params(("arbitrary",), zrows * LANES * 4),
        name="permute_rows",
    )(src_row, dst_row, zero_row, src)


CAST_ROWS = 512


def _expert_weight_copies(w_hbm, stage, sems, layer, e):
    return [pltpu.make_async_copy(w.at[layer, e], s, sems.at[k]) for k, (w, s) in enumerate(zip(w_hbm, stage))]


def _expert_ffn_kernel(te_ref, first_ref, next_ref, nt_ref, x_ref, wg_hbm, wu_hbm, wd_hbm, o_ref,
                       wg_st, wu_st, wd_st, wg_bf, wu_bf, wd_bf, sems, *, layer, tm):
    i = pl.program_id(0)
    stage = (wg_st, wu_st, wd_st)
    copies = lambda e: _expert_weight_copies((wg_hbm, wu_hbm, wd_hbm), stage, sems, layer, e)

    @pl.when(i == 0)
    def _():
        for cp in copies(te_ref[0]):
            cp.start()

    @pl.when((i < nt_ref[0]) & (first_ref[i] == 1))
    def _():
        for cp in copies(te_ref[i]):
            cp.wait()
        for st, bf in zip(stage, (wg_bf, wu_bf, wd_bf)):
            rows = st.shape[0]
            for r0 in range(0, rows, CAST_ROWS):
                r1 = min(r0 + CAST_ROWS, rows)
                bf[r0:r1, :] = st[r0:r1, :].astype(BF16)

        @pl.when(next_ref[i] >= 0)
        def _():
            for cp in copies(next_ref[i]):
                cp.start()

    @pl.when(i < nt_ref[0])
    def _():
        xa, xb = _unpack_bf16_pair(_load_slabs(x_ref, tm))
        xa, xb = xa.astype(BF16), xb.astype(BF16)
        half = xa.shape[1]
        a = _dot(xa, wg_bf[:half, :]) + _dot(xb, wg_bf[half:, :])
        b = _dot(xa, wu_bf[:half, :]) + _dot(xb, wu_bf[half:, :])
        hid = (a * _sigmoid(a) * b).astype(BF16)
        ya = _dot(hid, wd_bf[:, :half])
        yb = _dot(hid, wd_bf[:, half:])
        _store_slabs(o_ref, _pack_bf16_pair(ya, yb))

    @pl.when(i >= nt_ref[0])
    def _():
        o_ref[...] = jnp.zeros(o_ref.shape, o_ref.dtype)


def _expert_ffn(xs, r, tile_expert, tile_first, tile_next, n_tiles, w_gate, w_up, w_down, layer, tm):
    p = xs.shape[0] // r
    d, f = w_gate.shape[2:]
    hbm = pl.BlockSpec(memory_space=pl.ANY)
    tile = pl.BlockSpec((tm * r, LANES), lambda i, *_: (i, 0))
    return pl.pallas_call(
        functools.partial(_expert_ffn_kernel, layer=layer, tm=tm),
        grid_spec=pltpu.PrefetchScalarGridSpec(
            num_scalar_prefetch=4,
            grid=(p // tm,),
            in_specs=[tile, hbm, hbm, hbm],
            out_specs=tile,
            scratch_shapes=[pltpu.VMEM((d, f), F32), pltpu.VMEM((d, f), F32), pltpu.VMEM((f, d), F32),
                            pltpu.VMEM((d, f), BF16), pltpu.VMEM((d, f), BF16), pltpu.VMEM((f, d), BF16),
                            pltpu.SemaphoreType.DMA((3,))],
        ),
        out_shape=jax.ShapeDtypeStruct((p * r, LANES), U32),
        compiler_params=_params(("arbitrary",), 3 * d * f * 6 + 4 * tm * r * LANES * 4 + 6 * tm * d * 4),
        name="expert_ffn",
    )(tile_expert, tile_first, tile_next, n_tiles, xs, w_gate, w_up, w_down)


def _moe_sparse(hp, r, logits, w_gate, w_up, w_down, layer, n_groups, per_group):
    t = hp.shape[0] // r
    n_exp = n_groups * per_group
    tm = 256 if t * TOP_K >= 256 * n_exp else 8
    meta, wts, cnt = _route(logits, n_groups, per_group)

    counts = cnt[0, n_groups:n_groups + n_exp]
    tiles_e = (counts + tm - 1) // tm
    tile_end = jnp.cumsum(tiles_e)
    base = (tile_end - tiles_e) * tm
    n_tiles = tile_end[-1]
    max_tiles = (t * TOP_K) // tm + n_exp
    all_tiles = jnp.arange(max_tiles, dtype=I32)
    tile_expert = jnp.searchsorted(tile_end, jnp.minimum(all_tiles, n_tiles - 1), side="right").astype(I32)
    slot = (base[meta[:, 0:TOP_K]] + meta[:, TOP_K:2 * TOP_K]).astype(I32)
    tok = jnp.arange(t, dtype=I32)
    zero_row = jnp.concatenate([jnp.where(counts % tm != 0, base + (tiles_e - 1) * tm, -1),
                                jnp.where(all_tiles >= n_tiles, all_tiles * tm, -1)]).astype(I32)

    prev_expert = jnp.concatenate([jnp.full((1,), -1, I32), tile_expert[:-1]])
    tile_first = ((all_tiles < n_tiles) & (tile_expert != prev_expert)).astype(I32)
    group_end = tile_end[tile_expert]
    tile_next = jnp.where(group_end < n_tiles, tile_expert[jnp.minimum(group_end, max_tiles - 1)], -1).astype(I32)

    xs = _permute_rows(hp, r, jnp.repeat(tok, TOP_K), slot.reshape(-1), max_tiles * tm, zero_row, tm)
    ys = _expert_ffn(xs, r, tile_expert, tile_first, tile_next, n_tiles.reshape(1).astype(I32),
                     w_gate, w_up, w_down, layer, tm)
    yk = _permute_rows(ys, r, slot.T.reshape(-1),
                       jnp.tile(tok, TOP_K) + jnp.repeat(jnp.arange(TOP_K, dtype=I32) * t, t), TOP_K * t)
    return yk.reshape(TOP_K, t * r, LANES), wts


def kernel(x, c, w_cond, w_mod, b_mod, w_in, b_forget, conv_w, g_conv_out, g_attn_out, w_out, ln1_g, ln1_b, w_router_group, b_router_group, w_router_expert, b_router_expert, w_gate, w_up, w_down, ln2_g, ln2_b):
    bsz, seq, d = x.shape
    depth = w_mod.shape[0]
    d_conv = conv_w.shape[2]
    n_heads = b_forget.shape[1]
    d_att = n_heads * HEAD_DIM
    n_main = 3 * d_conv + 3 * d_att
    n_groups = w_router_group.shape[2]
    n_exp = w_router_expert.shape[2]
    alpha = (2 * depth) ** 0.25
    assert w_in.shape[2] == n_main + n_heads and n_groups + n_exp <= LANES
    assert d_conv % LANES == 0 and (d // 2) % LANES == 0

    mod = _modulation(c, w_cond, w_mod, b_mod).reshape(depth, bsz, 6, 1, d)
    part = lambda l, k: mod[l, :, k]
    pad = LANES - n_groups - n_exp
    w_router = jnp.pad(jnp.concatenate([w_router_group, w_router_expert], axis=2), ((0, 0), (0, 0), (0, pad)))
    b_router = jnp.pad(jnp.concatenate([b_router_group, b_router_expert], axis=1), ((0, 0), (0, pad)))
    w_in_t = jnp.swapaxes(w_in, 1, 2)
    q_cols = (3 * d_conv, 3 * d_conv + d_att)
    q_scale = HEAD_DIM ** -0.5 * LOG2E

    x2 = x.reshape(bsz * seq, d)
    h = _modulate(x2, part(0, 1), part(0, 0), seq)
    for l in range(depth):
        proj = _inproj(h, w_in_t, l, n_main, q_cols, q_scale)
        cum = _forget_cumsum(h, w_in_t[l, n_main:], b_forget[l], bsz, seq)
        y_conv = _gated_conv(proj, conv_w[l], g_conv_out[l], seq)
        o_att = _attention(proj, cum, bsz, seq, 3 * d_conv // LANES, n_heads)
        y = _outproj(y_conv, o_att, g_attn_out[l], w_out, l)
        x2, hp, logits = _ln_after_mixer(x2, y, part(l, 2), ln1_g[l], ln1_b[l], part(l, 4), part(l, 3),
                                         w_router[l], b_router[l].reshape(1, LANES), seq, alpha)
        yk, wts = _moe_sparse(hp, d // 2 // LANES, logits, w_gate, w_up, w_down, l, n_groups, n_exp // n_groups)
        if l + 1 < depth:
            x2, h = _ln_after_moe(x2, yk, wts, part(l, 5), ln2_g[l], ln2_b[l], seq, alpha,
                                  nxt=(part(l + 1, 1), part(l + 1, 0)))
        else:
            (x2,) = _ln_after_moe(x2, yk, wts, part(l, 5), ln2_g[l], ln2_b[l], seq, alpha)
    return x2.reshape(bsz, seq, d)
```

```python
import functools
import math

import jax
import jax.numpy as jnp
from jax import lax
from jax.experimental import pallas as pl
from jax.experimental.pallas import tpu as pltpu

F32 = jnp.float32
BF16 = jnp.bfloat16
U32 = jnp.uint32
I32 = jnp.int32

LANES = 128
SUBLANES = 8
VMEM_BYTES = 64 * 1024 * 1024
HEAD_DIM = 128
TOP_K = 2
LN_EPS = 1e-5
LOG2E = math.log2(math.e)
NEG_INF = float("-inf")
HI16 = 0xFFFF0000


def _tile(n, pref):
    t = min(n, pref)
    while n % t:
        t -= 1
    return t


def _params(semantics, vmem_bytes):
    limit = min(VMEM_BYTES - (4 << 20), vmem_bytes + (8 << 20))
    return pltpu.CompilerParams(dimension_semantics=semantics, vmem_limit_bytes=int(limit))


def _split2(a):
    hi = a.astype(BF16)
    lo = (a - hi.astype(F32)).astype(BF16)
    return hi, lo


def _dot(a, b):
    return jnp.dot(a, b, preferred_element_type=F32)


def _dot_nt(a, b):
    return lax.dot_general(a, b, (((1,), (1,)), ((), ())), preferred_element_type=F32)


def _dot_f32x3(a, b):
    ah, al = _split2(a)
    bh, bl = _split2(b)
    return _dot(ah, bh) + (_dot(al, bh) + _dot(ah, bl))


def _sigmoid(x):
    return 1.0 / (1.0 + jnp.exp(-x))


def _pack_bf16_pair(a, b):
    ua = lax.bitcast_convert_type(a.astype(BF16).astype(F32), U32)
    ub = lax.bitcast_convert_type(b.astype(BF16).astype(F32), U32)
    return ua | lax.shift_right_logical(ub, jnp.uint32(16))


def _unpack_bf16_pair(w):
    a = lax.bitcast_convert_type(w & jnp.uint32(HI16), F32)
    b = lax.bitcast_convert_type(lax.shift_left(w, jnp.uint32(16)), F32)
    return a, b


def _store_slabs(ref, val):
    rows, width = val.shape
    r = width // LANES
    for c in range(r):
        ref[pl.ds(c, rows, stride=r), :] = val[:, c * LANES:(c + 1) * LANES]


def _load_slabs(ref, rows):
    r = ref.shape[0] // rows
    return jnp.concatenate([ref[pl.ds(c, rows, stride=r), :] for c in range(r)], axis=1)


def _cond_embed_kernel(c_ref, w_ref, e_ref):
    c = c_ref[...]
    e_ref[...] = _dot_f32x3(c * _sigmoid(c), w_ref[...])


def _mod_kernel(e_ref, w_ref, b_ref, o_ref):
    o_ref[...] = _dot_f32x3(e_ref[...], w_ref[...]) + b_ref[...]


def _modulation(c, w_cond, w_mod, b_mod):
    bsz, d = c.shape
    depth, rank, n_mod = w_mod.shape
    rows = SUBLANES
    c_pad = jnp.pad(c, ((0, rows - bsz), (0, 0)))
    e = pl.pallas_call(
        _cond_embed_kernel,
        out_shape=jax.ShapeDtypeStruct((rows, rank), F32),
        compiler_params=_params(None, 8 * (rows * d + d * rank + rows * rank)),
        name="cond_embed",
    )(c_pad, w_cond)
    tn = _tile(n_mod, 4096)
    mod = pl.pallas_call(
        _mod_kernel,
        grid=(depth, n_mod // tn),
        in_specs=[
            pl.BlockSpec((rows, rank), lambda l, j: (0, 0)),
            pl.BlockSpec((None, rank, tn), lambda l, j: (l, 0, j)),
            pl.BlockSpec((None, 1, tn), lambda l, j: (l, 0, j)),
        ],
        out_specs=pl.BlockSpec((None, rows, tn), lambda l, j: (l, 0, j)),
        out_shape=jax.ShapeDtypeStruct((depth, rows, n_mod), F32),
        compiler_params=_params(("arbitrary", "arbitrary"), 8 * (rank * tn + rows * tn + tn + rows * rank)),
        name="adaln_mod",
    )(e, w_mod, b_mod.reshape(depth, 1, n_mod))
    return mod[:, :bsz]


def _modulate_kernel(x_ref, sc_ref, sh_ref, h_ref):
    h_ref[...] = (x_ref[...] * (1.0 + sc_ref[...]) + sh_ref[...]).astype(BF16)


def _modulate(x2, sc, sh, seq):
    t, d = x2.shape
    tm = _tile(seq, 512)
    per_b = seq // tm
    vec = pl.BlockSpec((None, 1, d), lambda i: (i // per_b, 0, 0))
    return pl.pallas_call(
        _modulate_kernel,
        grid=(t // tm,),
        in_specs=[pl.BlockSpec((tm, d), lambda i: (i, 0)), vec, vec],
        out_specs=pl.BlockSpec((tm, d), lambda i: (i, 0)),
        out_shape=jax.ShapeDtypeStruct((t, d), BF16),
        compiler_params=_params(("arbitrary",), tm * d * 12),
        name="modulate",
    )(x2, sc, sh)


def _inproj_kernel(h_ref, w_ref, o_ref, wbf_ref, *, q_tiles, q_scale):
    @pl.when(pl.program_id(1) == 0)
    def _():
        j = pl.program_id(0)
        scale = jnp.where((j >= q_tiles[0]) & (j < q_tiles[1]), q_scale, 1.0).astype(F32)
        wbf_ref[...] = (w_ref[...] * scale).astype(BF16)

    acc = _dot_nt(h_ref[...], wbf_ref[...])
    for cb in range(o_ref.shape[0]):
        o_ref[cb] = acc[:, cb * LANES:(cb + 1) * LANES].astype(BF16)


def _inproj(h, w_in_t, layer, n_main, q_cols, q_scale):
    t, d = h.shape
    tm = _tile(t, 1024)
    tn = _tile(math.gcd(q_cols[0], q_cols[1] - q_cols[0]), 512)
    q_tiles = (q_cols[0] // tn, q_cols[1] // tn)
    return pl.pallas_call(
        functools.partial(_inproj_kernel, q_tiles=q_tiles, q_scale=q_scale),
        grid=(n_main // tn, t // tm),
        in_specs=[
            pl.BlockSpec((tm, d), lambda j, i: (i, 0)),
            pl.BlockSpec((None, tn, d), lambda j, i: (layer, j, 0)),
        ],
        out_specs=pl.BlockSpec((tn // LANES, tm, LANES), lambda j, i: (j, i, 0)),
        out_shape=jax.ShapeDtypeStruct((n_main // LANES, t, LANES), BF16),
        scratch_shapes=[pltpu.VMEM((tn, d), BF16)],
        compiler_params=_params(("arbitrary", "arbitrary"), 2 * tm * d * 2 + d * tn * 10 + 3 * tm * tn * 4),
        name="in_proj",
    )(h, w_in_t)


def _forget_kernel(h_ref, w_ref, b_ref, o_ref, carry_ref):
    @pl.when(pl.program_id(1) == 0)
    def _():
        carry_ref[...] = jnp.zeros_like(carry_ref)

    h = h_ref[...]
    w_hi, w_lo = _split2(w_ref[...])
    f = _dot_nt(w_hi, h) + _dot_nt(w_lo, h) + b_ref[...]
    log_f = jnp.minimum(f, 0.0) - jnp.log(1.0 + jnp.exp(-jnp.abs(f)))
    nh, bs = log_f.shape
    row = lax.broadcasted_iota(I32, (bs, bs), 0)
    col = lax.broadcasted_iota(I32, (bs, bs), 1)
    triu = jnp.where(row <= col, 1.0, 0.0).astype(BF16)
    p0 = log_f.astype(BF16)
    r1 = log_f - p0.astype(F32)
    p1 = r1.astype(BF16)
    p2 = (r1 - p1.astype(F32)).astype(BF16)
    cs = _dot(p0, triu) + (_dot(p1, triu) + _dot(p2, triu)) + carry_ref[...]
    carry_ref[...] = cs[:, bs - 1:bs]
    o_ref[...] = cs


def _forget_cumsum(h, w_f_t, b_f, bsz, seq):
    t, d = h.shape
    nh = w_f_t.shape[0]
    bs = _tile(seq, 512)
    per_b = seq // bs
    return pl.pallas_call(
        _forget_kernel,
        grid=(bsz, per_b),
        in_specs=[
            pl.BlockSpec((bs, d), lambda b, i: (b * per_b + i, 0)),
            pl.BlockSpec((nh, d), lambda b, i: (0, 0)),
            pl.BlockSpec((nh, 1), lambda b, i: (0, 0)),
        ],
        out_specs=pl.BlockSpec((None, nh, bs), lambda b, i: (b, 0, i)),
        out_shape=jax.ShapeDtypeStruct((bsz, nh, seq), F32),
        scratch_shapes=[pltpu.VMEM((nh, 1), F32)],
        compiler_params=_params(("arbitrary", "arbitrary"), 2 * bs * d * 2 + nh * d * 16 + bs * bs * 8),
        name="forget_cumsum",
    )(h, w_f_t, b_f.reshape(nh, 1))


def _conv_kernel(bg_ref, cg_ref, hc_ref, w_ref, g_ref, o_ref, ubuf_ref, *, per_b, d_conv):
    i = pl.program_id(0)
    ts = bg_ref.shape[1]
    halo = SUBLANES

    @pl.when(i % per_b == 0)
    def _():
        ubuf_ref[:, 0:halo, :] = jnp.zeros((ubuf_ref.shape[0], halo, LANES), F32)

    @pl.when(i % per_b != 0)
    def _():
        ubuf_ref[:, 0:halo, :] = ubuf_ref[:, ts:ts + halo, :]

    u = cg_ref[...].astype(F32) * hc_ref[...].astype(F32)
    ubuf_ref[:, halo:halo + ts, :] = u
    u1 = ubuf_ref[:, halo - 1:halo - 1 + ts, :]
    u2 = ubuf_ref[:, halo - 2:halo - 2 + ts, :]
    conv = u2 * w_ref[0] + u1 * w_ref[1] + u * w_ref[2]
    y = bg_ref[...].astype(F32) * conv
    ssq = jnp.sum(jnp.sum(y * y, axis=0), axis=-1, keepdims=True)
    r = lax.rsqrt(ssq * (1.0 / d_conv) + LN_EPS)
    o_ref[...] = (y * r[None] * g_ref[...]).astype(BF16)


def _gated_conv(proj, conv_w, g_conv, seq):
    _, t, _ = proj.shape
    width, d_conv = conv_w.shape
    assert width == 3, "causal depthwise convolution of width 3"
    nc = d_conv // LANES
    ts = _tile(seq, 512)
    per_b = seq // ts
    slab = lambda k: pl.BlockSpec((nc, ts, LANES), lambda i: (k, i, 0))
    return pl.pallas_call(
        functools.partial(_conv_kernel, per_b=per_b, d_conv=d_conv),
        grid=(t // ts,),
        in_specs=[
            slab(0), slab(1), slab(2),
            pl.BlockSpec((width, nc, 1, LANES), lambda i: (0, 0, 0, 0)),
            pl.BlockSpec((nc, 1, LANES), lambda i: (0, 0, 0)),
        ],
        out_specs=pl.BlockSpec((nc, ts, LANES), lambda i: (0, i, 0)),
        out_shape=jax.ShapeDtypeStruct((nc, t, LANES), BF16),
        scratch_shapes=[pltpu.VMEM((nc, ts + 2 * SUBLANES, LANES), F32)],
        compiler_params=_params(("arbitrary",), nc * ts * LANES * (2 * 4 * 2 + 4 * 8)),
        name="gated_conv",
    )(proj, proj, proj, conv_w.reshape(width, nc, 1, LANES), g_conv.reshape(nc, 1, LANES))


def _attn_kernel(q_ref, k_ref, v_ref, ck_ref, o_ref, m_ref, l_ref, acc_ref, *, blk):
    qi = pl.program_id(2)
    n_heads = q_ref.shape[0]
    m_ref[...] = jnp.full(m_ref.shape, NEG_INF, F32)
    l_ref[...] = jnp.zeros(l_ref.shape, F32)
    acc_ref[...] = jnp.zeros(acc_ref.shape, F32)
    ones = jnp.ones((blk, LANES), BF16)

    def step(kb, masked):
        ks = pl.multiple_of(kb * blk, blk)
        for g in range(n_heads):
            k = k_ref[g, pl.ds(ks, blk), :]
            v = v_ref[g, pl.ds(ks, blk), :]
            t = _dot_nt(q_ref[g], k) - ck_ref[g, kb] * LOG2E
            if masked:
                row = lax.broadcasted_iota(I32, t.shape, 0)
                col = lax.broadcasted_iota(I32, t.shape, 1)
                t = jnp.where(col <= row, t, NEG_INF)
            m = m_ref[g]
            m_new = jnp.maximum(m, jnp.broadcast_to(jnp.max(t, axis=1, keepdims=True), m.shape))
            p = jnp.exp2(t - jnp.tile(m_new, (1, blk // LANES)))
            alpha = jnp.exp2(m - m_new)
            pv = _dot(p.astype(BF16), jnp.concatenate([v, ones], axis=1))
            l_ref[g] = alpha * l_ref[g] + pv[:, HEAD_DIM:]
            acc_ref[g] = alpha * acc_ref[g] + pv[:, :HEAD_DIM]
            m_ref[g] = m_new

    def body(kb, carry):
        step(kb, False)
        return carry

    lax.fori_loop(0, qi, body, 0)
    step(qi, True)
    for g in range(n_heads):
        o_ref[g] = (acc_ref[g] / l_ref[g]).astype(BF16)


def _attention(proj, cum, bsz, seq, q_slab, n_heads):
    n_slabs = proj.shape[0]
    blk = _tile(seq, 256)
    nq = seq // blk
    grp = _tile(math.gcd(n_heads, q_slab), 8)
    proj4 = proj.reshape(n_slabs, bsz, seq, LANES)
    ck = cum.reshape(bsz, n_heads, nq, 1, blk)
    kv = lambda off: pl.BlockSpec((grp, None, seq, LANES), lambda b, h, i: (off // grp + h, b, 0, 0))
    out = pl.pallas_call(
        functools.partial(_attn_kernel, blk=blk),
        grid=(bsz, n_heads // grp, nq),
        in_specs=[
            pl.BlockSpec((grp, None, blk, LANES), lambda b, h, i: (q_slab // grp + h, b, i, 0)),
            kv(q_slab + n_heads), kv(q_slab + 2 * n_heads),
            pl.BlockSpec((None, grp, nq, 1, blk), lambda b, h, i: (b, h, 0, 0, 0)),
        ],
        out_specs=pl.BlockSpec((grp, None, blk, LANES), lambda b, h, i: (h, b, i, 0)),
        out_shape=jax.ShapeDtypeStruct((n_heads, bsz, seq, LANES), BF16),
        scratch_shapes=[
            pltpu.VMEM((grp, blk, LANES), F32),
            pltpu.VMEM((grp, blk, LANES), F32),
            pltpu.VMEM((grp, blk, HEAD_DIM), F32),
        ],
        compiler_params=_params(("arbitrary", "arbitrary", "arbitrary"),
                                grp * (4 * seq * LANES * 2 + 2 * seq * 4 * SUBLANES + 4 * blk * LANES * 2
                                       + 2 * blk * LANES * 4 + blk * HEAD_DIM * 4 + 8 * blk * blk * 4)),
        name="fox_attention",
    )(proj4, proj4, proj4, ck)
    return out.reshape(n_heads, bsz * seq, LANES)


def _outproj_kernel(yc_ref, oa_ref, ga_ref, w_ref, o_ref, ybuf_ref, *, d_att):
    nc, na = yc_ref.shape[0], oa_ref.shape[0]

    @pl.when(pl.program_id(1) == 0)
    def _():
        for cb in range(nc):
            ybuf_ref[:, cb * LANES:(cb + 1) * LANES] = yc_ref[cb]
        oa = oa_ref[...].astype(F32)
        ssq = jnp.sum(jnp.sum(oa * oa, axis=0), axis=-1, keepdims=True)
        r = lax.rsqrt(ssq * (1.0 / d_att) + LN_EPS)
        for cb in range(na):
            ybuf_ref[:, (nc + cb) * LANES:(nc + cb + 1) * LANES] = (oa[cb] * r * ga_ref[cb]).astype(BF16)

    o_ref[...] = _dot(ybuf_ref[...], w_ref[...].astype(BF16)).astype(BF16)


def _outproj(y_conv, o_att, g_attn, w_out, layer):
    nc, t, _ = y_conv.shape
    na = o_att.shape[0]
    d_mix, d = w_out.shape[1:]
    tm = _tile(t, 1024)
    tn = _tile(d, 512)
    return pl.pallas_call(
        functools.partial(_outproj_kernel, d_att=na * LANES),
        grid=(t // tm, d // tn),
        in_specs=[
            pl.BlockSpec((nc, tm, LANES), lambda i, j: (0, i, 0)),
            pl.BlockSpec((na, tm, LANES), lambda i, j: (0, i, 0)),
            pl.BlockSpec((na, 1, LANES), lambda i, j: (0, 0, 0)),
            pl.BlockSpec((None, d_mix, tn), lambda i, j: (layer, 0, j)),
        ],
        out_specs=pl.BlockSpec((tm, tn), lambda i, j: (i, j)),
        out_shape=jax.ShapeDtypeStruct((t, d), BF16),
        scratch_shapes=[pltpu.VMEM((tm, d_mix), BF16)],
        compiler_params=_params(("arbitrary", "arbitrary"),
                                2 * tm * d_mix * 2 + tm * d_mix * 2 + d_mix * tn * 10 + 3 * tm * tn * 4),
        name="out_proj",
    )(y_conv, o_att, g_attn.reshape(na, 1, LANES), w_out)


def _ln_finish(z, g_ref, b_ref):
    mu = jnp.mean(z, axis=-1, keepdims=True)
    zc = z - mu
    var = jnp.mean(zc * zc, axis=-1, keepdims=True)
    return zc * lax.rsqrt(var + LN_EPS) * g_ref[...] + b_ref[...]


def _ln_mixer_kernel(x_ref, y_ref, gt_ref, g_ref, b_ref, sc_ref, sh_ref, wr_ref, br_ref,
                     xo_ref, hp_ref, lg_ref, *, alpha):
    xn = _ln_finish(alpha * x_ref[...] + (1.0 + gt_ref[...]) * y_ref[...].astype(F32), g_ref, b_ref)
    h = xn * (1.0 + sc_ref[...]) + sh_ref[...]
    half = h.shape[1] // 2
    xo_ref[...] = xn
    _store_slabs(hp_ref, _pack_bf16_pair(h[:, :half], h[:, half:]))
    lg_ref[...] = _dot_f32x3(h, wr_ref[...]) + br_ref[...]


def _ln_moe_kernel(*refs, alpha, with_h):
    x_ref, y0_ref, y1_ref, wt_ref, gt_ref, g_ref, b_ref = refs[:7]
    wt = wt_ref[...]
    w0, w1 = wt[:, 0:1], wt[:, 1:2]
    tm = x_ref.shape[0]
    a0, b0 = _unpack_bf16_pair(_load_slabs(y0_ref, tm))
    a1, b1 = _unpack_bf16_pair(_load_slabs(y1_ref, tm))
    y = jnp.concatenate([w0 * a0 + w1 * a1, w0 * b0 + w1 * b1], axis=1)
    xn = _ln_finish(alpha * x_ref[...] + (1.0 + gt_ref[...]) * y, g_ref, b_ref)
    if with_h:
        sc_ref, sh_ref, xo_ref, ho_ref = refs[7:]
        ho_ref[...] = (xn * (1.0 + sc_ref[...]) + sh_ref[...]).astype(BF16)
    else:
        (xo_ref,) = refs[7:]
    xo_ref[...] = xn


def _ln_specs(t, d, seq, tm):
    per_b = seq // tm
    tok = pl.BlockSpec((tm, d), lambda i: (i, 0))
    vec = pl.BlockSpec((None, 1, d), lambda i: (i // per_b, 0, 0))
    row = pl.BlockSpec((1, d), lambda i: (0, 0))
    return tok, vec, row


def _ln_after_mixer(x2, y, gt, g, b, sc, sh, w_router, b_router, seq, alpha):
    t, d = x2.shape
    tm = _tile(seq, 256)
    tok, vec, row = _ln_specs(t, d, seq, tm)
    r = d // 2 // LANES
    return pl.pallas_call(
        functools.partial(_ln_mixer_kernel, alpha=alpha),
        grid=(t // tm,),
        in_specs=[tok, tok, vec, row, row, vec, vec,
                  pl.BlockSpec((d, LANES), lambda i: (0, 0)), pl.BlockSpec((1, LANES), lambda i: (0, 0))],
        out_specs=[tok, pl.BlockSpec((tm * r, LANES), lambda i: (i, 0)), pl.BlockSpec((tm, LANES), lambda i: (i, 0))],
        out_shape=[jax.ShapeDtypeStruct((t, d), F32), jax.ShapeDtypeStruct((t * r, LANES), U32),
                   jax.ShapeDtypeStruct((t, LANES), F32)],
        compiler_params=_params(("arbitrary",), tm * d * 44 + d * LANES * 16),
        name="ln_mixer",
    )(x2, y, gt, g.reshape(1, d), b.reshape(1, d), sc, sh, w_router, b_router)


def _ln_after_moe(x2, yk, wts, gt, g, b, seq, alpha, nxt=None):
    t, d = x2.shape
    tm = _tile(seq, 256)
    tok, vec, row = _ln_specs(t, d, seq, tm)
    r = yk.shape[1] // t
    yk_spec = lambda k: pl.BlockSpec((None, tm * r, LANES), lambda i: (k, i, 0))
    ins = [x2, yk, yk, wts, gt, g.reshape(1, d), b.reshape(1, d)]
    in_specs = [tok, yk_spec(0), yk_spec(1), pl.BlockSpec((tm, LANES), lambda i: (i, 0)), vec, row, row]
    outs, out_specs = [jax.ShapeDtypeStruct((t, d), F32)], [tok]
    if nxt is not None:
        ins += list(nxt)
        in_specs += [vec, vec]
        outs.append(jax.ShapeDtypeStruct((t, d), BF16))
        out_specs.append(tok)
    return pl.pallas_call(
        functools.partial(_ln_moe_kernel, alpha=alpha, with_h=nxt is not None),
        grid=(t // tm,),
        in_specs=in_specs,
        out_specs=out_specs,
        out_shape=outs,
        compiler_params=_params(("arbitrary",), tm * d * 44),
        name="ln_moe",
    )(*ins)


def _route_kernel(lg_ref, meta_ref, wts_ref, cnt_ref, carry_ref, *, n_groups, per_group):
    @pl.when(pl.program_id(0) == 0)
    def _():
        carry_ref[...] = jnp.zeros_like(carry_ref)

    lg = lg_ref[...]
    tb = lg.shape[0]
    lane = lax.broadcasted_iota(I32, lg.shape, 1)
    n_exp = n_groups * per_group

    def first_argmax(vals, vmax):
        return jnp.min(jnp.where(vals == vmax, lane, LANES), axis=1, keepdims=True)

    gl = jnp.where(lane < n_groups, lg, NEG_INF)
    g_max = jnp.max(gl, axis=1, keepdims=True)
    g_val = 1.0 / jnp.sum(jnp.exp(gl - g_max), axis=1, keepdims=True)
    g_idx = first_argmax(gl, g_max)
    lo = n_groups + g_idx * per_group
    el = jnp.where((lane >= lo) & (lane < lo + per_group) & (lane < n_groups + n_exp), lg, NEG_INF)
    e1 = jnp.max(el, axis=1, keepdims=True)
    i1 = first_argmax(el, e1)
    el2 = jnp.where(lane == i1, NEG_INF, el)
    e2 = jnp.max(el2, axis=1, keepdims=True)
    i2 = first_argmax(el2, e2)
    d = jnp.exp(e2 - e1)
    w1 = g_val / (1.0 + d)
    w2 = g_val * d / (1.0 + d)
    wts_ref[...] = jnp.where(lane == 0, w1, jnp.where(lane == 1, w2, 0.0))

    sel = jnp.where((lane == i1) | (lane == i2), 1.0, 0.0)
    row = lax.broadcasted_iota(I32, (tb, tb), 0)
    col = lax.broadcasted_iota(I32, (tb, tb), 1)
    strict_lower = jnp.where(col < row, 1.0, 0.0).astype(BF16)
    rank = _dot(strict_lower, sel.astype(BF16)) + carry_ref[...]
    pos1 = jnp.sum(jnp.where(lane == i1, rank, 0.0), axis=1, keepdims=True).astype(I32)
    pos2 = jnp.sum(jnp.where(lane == i2, rank, 0.0), axis=1, keepdims=True).astype(I32)
    total = rank[tb - 1:tb, :] + sel[tb - 1:tb, :]
    carry_ref[...] = total
    cnt_ref[...] = total.astype(I32)
    meta_ref[...] = jnp.where(lane == 0, i1 - n_groups, jnp.where(lane == 1, i2 - n_groups,
                              jnp.where(lane == 2, pos1, jnp.where(lane == 3, pos2, 0))))


def _route(logits, n_groups, per_group):
    t = logits.shape[0]
    tb = _tile(t, 512)
    blk = pl.BlockSpec((tb, LANES), lambda i: (i, 0))
    one = pl.BlockSpec((1, LANES), lambda i: (0, 0))
    return pl.pallas_call(
        functools.partial(_route_kernel, n_groups=n_groups, per_group=per_group),
        grid=(t // tb,),
        in_specs=[blk],
        out_specs=[blk, blk, one],
        out_shape=[jax.ShapeDtypeStruct((t, LANES), I32), jax.ShapeDtypeStruct((t, LANES), F32),
                   jax.ShapeDtypeStruct((1, LANES), I32)],
        scratch_shapes=[pltpu.VMEM((1, LANES), F32)],
        compiler_params=_params(("arbitrary",), tb * LANES * 4 * 24 + tb * tb * 8),
        name="route",
    )(logits)


SCALAR_UNROLL = 8


def _gather_rows_kernel(idx_ref, trips_ref, src_ref, dst_ref, vbuf_ref, row_ref, gsem, osem, *, tile, r, fanout):
    n_tiles = trips_ref[2]
    if fanout:
        def clear(i, c):
            for u in range(SCALAR_UNROLL):
                row_ref[i * SCALAR_UNROLL + u] = 0
            return c

        def put(i, c):
            for u in range(SCALAR_UNROLL):
                j = i * SCALAR_UNROLL + u
                row_ref[idx_ref[j]] = lax.div(j, fanout)
            return c

        lax.fori_loop(0, lax.div(trips_ref[0], SCALAR_UNROLL), clear, 0)
        lax.fori_loop(0, lax.div(trips_ref[1], SCALAR_UNROLL), put, 0)
        rows = row_ref
    else:
        rows = idx_ref

    def gather_start(ti, slot):
        def one(j, c):
            s = pl.multiple_of(rows[ti * tile + j] * r, r)
            d = pl.multiple_of(j * r, r)
            pltpu.make_async_copy(src_ref.at[pl.ds(s, r)], vbuf_ref.at[slot, pl.ds(d, r)], gsem.at[slot]).start()
            return c

        lax.fori_loop(0, tile, one, 0, unroll=8)

    def gather_wait(slot):
        pltpu.make_async_copy(src_ref.at[pl.ds(0, tile * r)], vbuf_ref.at[slot], gsem.at[slot]).wait()

    def tile_out(ti, slot):
        start = pl.multiple_of(ti * (tile * r), tile * r)
        return pltpu.make_async_copy(vbuf_ref.at[slot], dst_ref.at[pl.ds(start, tile * r)], osem.at[slot])

    gather_start(0, 0)

    def step(ti, c):
        slot = ti % 2

        @pl.when(ti >= 1)
        def _():
            tile_out(ti - 1, 1 - slot).wait()

        @pl.when(ti + 1 < n_tiles)
        def _():
            gather_start(ti + 1, 1 - slot)

        gather_wait(slot)
        tile_out(ti, slot).start()
        return c

    lax.fori_loop(0, n_tiles, step, 0)
    tile_out(n_tiles - 1, (n_tiles - 1) % 2).wait()


GATHER_TILE = 256


def _gather_rows(src, r, idx, n_dst, fanout=0):
    tile = _tile(n_dst, GATHER_TILE)
    assert src.shape[0] >= tile * r and n_dst % SCALAR_UNROLL == 0 and idx.shape[0] % SCALAR_UNROLL == 0
    trips = jnp.array([n_dst, idx.shape[0], n_dst // tile], I32)
    return pl.pallas_call(
        functools.partial(_gather_rows_kernel, tile=tile, r=r, fanout=fanout),
        grid_spec=pltpu.PrefetchScalarGridSpec(
            num_scalar_prefetch=2,
            grid=(1,),
            in_specs=[pl.BlockSpec(memory_space=pl.ANY)],
            out_specs=pl.BlockSpec(memory_space=pl.ANY),
            scratch_shapes=[pltpu.VMEM((2, tile * r, LANES), src.dtype),
                            pltpu.SMEM((n_dst if fanout else 1,), I32),
                            pltpu.SemaphoreType.DMA((2,)), pltpu.SemaphoreType.DMA((2,))],
        ),
        out_shape=jax.ShapeDtypeStruct((n_dst * r, LANES), src.dtype),
        compiler_params=_params(("arbitrary",), 2 * tile * r * LANES * 4),
        name="gather_rows",
    )(idx, trips, src)


CAST_ROWS = 512


def _expert_weight_copies(w_hbm, stage, sems, layer, e):
    return [pltpu.make_async_copy(w.at[layer, e], s, sems.at[k]) for k, (w, s) in enumerate(zip(w_hbm, stage))]


def _expert_ffn_kernel(te_ref, first_ref, next_ref, nt_ref, x_ref, wg_hbm, wu_hbm, wd_hbm, o_ref,
                       wg_st, wu_st, wd_st, wg_bf, wu_bf, wd_bf, sems, *, layer, tm):
    i = pl.program_id(0)
    stage = (wg_st, wu_st, wd_st)
    copies = lambda e: _expert_weight_copies((wg_hbm, wu_hbm, wd_hbm), stage, sems, layer, e)

    @pl.when(i == 0)
    def _():
        for cp in copies(te_ref[0]):
            cp.start()

    @pl.when((i < nt_ref[0]) & (first_ref[i] == 1))
    def _():
        for cp in copies(te_ref[i]):
            cp.wait()
        for st, bf in zip(stage, (wg_bf, wu_bf, wd_bf)):
            rows = st.shape[0]
            for r0 in range(0, rows, CAST_ROWS):
                r1 = min(r0 + CAST_ROWS, rows)
                bf[r0:r1, :] = st[r0:r1, :].astype(BF16)

        @pl.when(next_ref[i] >= 0)
        def _():
            for cp in copies(next_ref[i]):
                cp.start()

    @pl.when(i < nt_ref[0])
    def _():
        xa, xb = _unpack_bf16_pair(_load_slabs(x_ref, tm))
        xa, xb = xa.astype(BF16), xb.astype(BF16)
        half = xa.shape[1]
        a = _dot(xa, wg_bf[:half, :]) + _dot(xb, wg_bf[half:, :])
        b = _dot(xa, wu_bf[:half, :]) + _dot(xb, wu_bf[half:, :])
        hid = (a * _sigmoid(a) * b).astype(BF16)
        ya = _dot(hid, wd_bf[:, :half])
        yb = _dot(hid, wd_bf[:, half:])
        _store_slabs(o_ref, _pack_bf16_pair(ya, yb))

    @pl.when(i >= nt_ref[0])
    def _():
        o_ref[...] = jnp.zeros(o_ref.shape, o_ref.dtype)


def _expert_ffn(xs, r, tile_expert, tile_first, tile_next, n_tiles, w_gate, w_up, w_down, layer, tm):
    p = xs.shape[0] // r
    d, f = w_gate.shape[2:]
    hbm = pl.BlockSpec(memory_space=pl.ANY)
    tile = pl.BlockSpec((tm * r, LANES), lambda i, *_: (i, 0))
    return pl.pallas_call(
        functools.partial(_expert_ffn_kernel, layer=layer, tm=tm),
        grid_spec=pltpu.PrefetchScalarGridSpec(
            num_scalar_prefetch=4,
            grid=(p // tm,),
            in_specs=[tile, hbm, hbm, hbm],
            out_specs=tile,
            scratch_shapes=[pltpu.VMEM((d, f), F32), pltpu.VMEM((d, f), F32), pltpu.VMEM((f, d), F32),
                            pltpu.VMEM((d, f), BF16), pltpu.VMEM((d, f), BF16), pltpu.VMEM((f, d), BF16),
                            pltpu.SemaphoreType.DMA((3,))],
        ),
        out_shape=jax.ShapeDtypeStruct((p * r, LANES), U32),
        compiler_params=_params(("arbitrary",), 3 * d * f * 6 + 4 * tm * r * LANES * 4 + 6 * tm * d * 4),
        name="expert_ffn",
    )(tile_expert, tile_first, tile_next, n_tiles, xs, w_gate, w_up, w_down)


def _moe_sparse(hp, r, logits, w_gate, w_up, w_down, layer, n_groups, per_group):
    t = hp.shape[0] // r
    n_exp = n_groups * per_group
    tm = 256 if t * TOP_K >= 256 * n_exp else 8
    meta, wts, cnt = _route(logits, n_groups, per_group)

    counts = cnt[0, n_groups:n_groups + n_exp]
    tiles_e = (counts + tm - 1) // tm
    tile_end = jnp.cumsum(tiles_e)
    base = (tile_end - tiles_e) * tm
    n_tiles = tile_end[-1]
    max_tiles = (t * TOP_K) // tm + n_exp
    all_tiles = jnp.arange(max_tiles, dtype=I32)
    tile_expert = jnp.searchsorted(tile_end, jnp.minimum(all_tiles, n_tiles - 1), side="right").astype(I32)
    slot = (base[meta[:, 0:TOP_K]] + meta[:, TOP_K:2 * TOP_K]).astype(I32)

    prev_expert = jnp.concatenate([jnp.full((1,), -1, I32), tile_expert[:-1]])
    tile_first = ((all_tiles < n_tiles) & (tile_expert != prev_expert)).astype(I32)
    group_end = tile_end[tile_expert]
    tile_next = jnp.where(group_end < n_tiles, tile_expert[jnp.minimum(group_end, max_tiles - 1)], -1).astype(I32)

    n_tiles = n_tiles.reshape(1).astype(I32)
    xs = _gather_rows(hp, r, slot.reshape(-1), max_tiles * tm, fanout=TOP_K)
    ys = _expert_ffn(xs, r, tile_expert, tile_first, tile_next, n_tiles, w_gate, w_up, w_down, layer, tm)
    yk = _gather_rows(ys, r, slot.T.reshape(-1), TOP_K * t)
    return yk.reshape(TOP_K, t * r, LANES), wts


def kernel(x, c, w_cond, w_mod, b_mod, w_in, b_forget, conv_w, g_conv_out, g_attn_out, w_out, ln1_g, ln1_b, w_router_group, b_router_group, w_router_expert, b_router_expert, w_gate, w_up, w_down, ln2_g, ln2_b):
    bsz, seq, d = x.shape
    depth = w_mod.shape[0]
    d_conv = conv_w.shape[2]
    n_heads = b_forget.shape[1]
    d_att = n_heads * HEAD_DIM
    n_main = 3 * d_conv + 3 * d_att
    n_groups = w_router_group.shape[2]
    n_exp = w_router_expert.shape[2]
    alpha = (2 * depth) ** 0.25
    assert w_in.shape[2] == n_main + n_heads and n_groups + n_exp <= LANES
    assert d_conv % LANES == 0 and (d // 2) % LANES == 0

    mod = _modulation(c, w_cond, w_mod, b_mod).reshape(depth, bsz, 6, 1, d)
    part = lambda l, k: mod[l, :, k]
    pad = LANES - n_groups - n_exp
    w_router = jnp.pad(jnp.concatenate([w_router_group, w_router_expert], axis=2), ((0, 0), (0, 0), (0, pad)))
    b_router = jnp.pad(jnp.concatenate([b_router_group, b_router_expert], axis=1), ((0, 0), (0, pad)))
    w_in_t = jnp.swapaxes(w_in, 1, 2)
    q_cols = (3 * d_conv, 3 * d_conv + d_att)
    q_scale = HEAD_DIM ** -0.5 * LOG2E

    x2 = x.reshape(bsz * seq, d)
    h = _modulate(x2, part(0, 1), part(0, 0), seq)
    for l in range(depth):
        proj = _inproj(h, w_in_t, l, n_main, q_cols, q_scale)
        cum = _forget_cumsum(h, w_in_t[l, n_main:], b_forget[l], bsz, seq)
        y_conv = _gated_conv(proj, conv_w[l], g_conv_out[l], seq)
        o_att = _attention(proj, cum, bsz, seq, 3 * d_conv // LANES, n_heads)
        y = _outproj(y_conv, o_att, g_attn_out[l], w_out, l)
        x2, hp, logits = _ln_after_mixer(x2, y, part(l, 2), ln1_g[l], ln1_b[l], part(l, 4), part(l, 3),
                                         w_router[l], b_router[l].reshape(1, LANES), seq, alpha)
        yk, wts = _moe_sparse(hp, d // 2 // LANES, logits, w_gate, w_up, w_down, l, n_groups, n_exp // n_groups)
        if l + 1 < depth:
            x2, h = _ln_after_moe(x2, yk, wts, part(l, 5), ln2_g[l], ln2_b[l], seq, alpha,
                                  nxt=(part(l + 1, 1), part(l + 1, 0)))
        else:
            (x2,) = _ln_after_moe(x2, yk, wts, part(l, 5), ln2_g[l], ln2_b[l], seq, alpha)
    return x2.reshape(bsz, seq, d)
```

```python
import functools
import math

import jax
import jax.numpy as jnp
from jax import lax
from jax.experimental import pallas as pl
from jax.experimental.pallas import tpu as pltpu

F32 = jnp.float32
BF16 = jnp.bfloat16
U32 = jnp.uint32
I32 = jnp.int32

LANES = 128
SUBLANES = 8
VMEM_BYTES = 64 * 1024 * 1024
HEAD_DIM = 128
TOP_K = 2
LN_EPS = 1e-5
LOG2E = math.log2(math.e)
NEG_INF = float("-inf")
HI16 = 0xFFFF0000


def _tile(n, pref):
    t = min(n, pref)
    while n % t:
        t -= 1
    return t


def _params(semantics, vmem_bytes):
    limit = min(VMEM_BYTES - (4 << 20), vmem_bytes + (8 << 20))
    return pltpu.CompilerParams(dimension_semantics=semantics, vmem_limit_bytes=int(limit))


def _split2(a):
    hi = a.astype(BF16)
    lo = (a - hi.astype(F32)).astype(BF16)
    return hi, lo


def _dot(a, b):
    return jnp.dot(a, b, preferred_element_type=F32)


def _dot_nt(a, b):
    return lax.dot_general(a, b, (((1,), (1,)), ((), ())), preferred_element_type=F32)


def _dot_f32x3(a, b):
    ah, al = _split2(a)
    bh, bl = _split2(b)
    return _dot(ah, bh) + (_dot(al, bh) + _dot(ah, bl))


def _sigmoid(x):
    return 1.0 / (1.0 + jnp.exp(-x))


def _pack_bf16_pair(a, b):
    ua = lax.bitcast_convert_type(a.astype(BF16).astype(F32), U32)
    ub = lax.bitcast_convert_type(b.astype(BF16).astype(F32), U32)
    return ua | lax.shift_right_logical(ub, jnp.uint32(16))


def _unpack_bf16_pair(w):
    a = lax.bitcast_convert_type(w & jnp.uint32(HI16), F32)
    b = lax.bitcast_convert_type(lax.shift_left(w, jnp.uint32(16)), F32)
    return a, b


def _store_slabs(ref, val):
    rows, width = val.shape
    r = width // LANES
    for c in range(r):
        ref[pl.ds(c, rows, stride=r), :] = val[:, c * LANES:(c + 1) * LANES]


def _load_slabs(ref, rows):
    r = ref.shape[0] // rows
    return jnp.concatenate([ref[pl.ds(c, rows, stride=r), :] for c in range(r)], axis=1)


def _cond_embed_kernel(c_ref, w_ref, e_ref):
    c = c_ref[...]
    e_ref[...] = _dot_f32x3(c * _sigmoid(c), w_ref[...])


def _mod_kernel(e_ref, w_ref, b_ref, o_ref):
    o_ref[...] = _dot_f32x3(e_ref[...], w_ref[...]) + b_ref[...]


def _modulation(c, w_cond, w_mod, b_mod):
    bsz, d = c.shape
    depth, rank, n_mod = w_mod.shape
    rows = SUBLANES
    c_pad = jnp.pad(c, ((0, rows - bsz), (0, 0)))
    e = pl.pallas_call(
        _cond_embed_kernel,
        out_shape=jax.ShapeDtypeStruct((rows, rank), F32),
        compiler_params=_params(None, 8 * (rows * d + d * rank + rows * rank)),
        name="cond_embed",
    )(c_pad, w_cond)
    tn = _tile(n_mod, 4096)
    mod = pl.pallas_call(
        _mod_kernel,
        grid=(depth, n_mod // tn),
        in_specs=[
            pl.BlockSpec((rows, rank), lambda l, j: (0, 0)),
            pl.BlockSpec((None, rank, tn), lambda l, j: (l, 0, j)),
            pl.BlockSpec((None, 1, tn), lambda l, j: (l, 0, j)),
        ],
        out_specs=pl.BlockSpec((None, rows, tn), lambda l, j: (l, 0, j)),
        out_shape=jax.ShapeDtypeStruct((depth, rows, n_mod), F32),
        compiler_params=_params(("arbitrary", "arbitrary"), 8 * (rank * tn + rows * tn + tn + rows * rank)),
        name="adaln_mod",
    )(e, w_mod, b_mod.reshape(depth, 1, n_mod))
    return mod[:, :bsz]


def _modulate_kernel(x_ref, sc_ref, sh_ref, h_ref):
    h_ref[...] = (x_ref[...] * (1.0 + sc_ref[...]) + sh_ref[...]).astype(BF16)


def _modulate(x2, sc, sh, seq):
    t, d = x2.shape
    tm = _tile(seq, 512)
    per_b = seq // tm
    vec = pl.BlockSpec((None, 1, d), lambda i: (i // per_b, 0, 0))
    return pl.pallas_call(
        _modulate_kernel,
        grid=(t // tm,),
        in_specs=[pl.BlockSpec((tm, d), lambda i: (i, 0)), vec, vec],
        out_specs=pl.BlockSpec((tm, d), lambda i: (i, 0)),
        out_shape=jax.ShapeDtypeStruct((t, d), BF16),
        compiler_params=_params(("arbitrary",), tm * d * 12),
        name="modulate",
    )(x2, sc, sh)


def _inproj_kernel(h_ref, w_ref, o_ref, wbf_ref, *, q_tiles, q_scale):
    @pl.when(pl.program_id(1) == 0)
    def _():
        j = pl.program_id(0)
        scale = jnp.where((j >= q_tiles[0]) & (j < q_tiles[1]), q_scale, 1.0).astype(F32)
        wbf_ref[...] = (w_ref[...] * scale).astype(BF16)

    acc = _dot_nt(h_ref[...], wbf_ref[...])
    for cb in range(o_ref.shape[0]):
        o_ref[cb] = acc[:, cb * LANES:(cb + 1) * LANES].astype(BF16)


def _inproj(h, w_in_t, layer, n_main, q_cols, q_scale):
    t, d = h.shape
    tm = _tile(t, 1024)
    tn = _tile(math.gcd(q_cols[0], q_cols[1] - q_cols[0]), 512)
    q_tiles = (q_cols[0] // tn, q_cols[1] // tn)
    return pl.pallas_call(
        functools.partial(_inproj_kernel, q_tiles=q_tiles, q_scale=q_scale),
        grid=(n_main // tn, t // tm),
        in_specs=[
            pl.BlockSpec((tm, d), lambda j, i: (i, 0)),
            pl.BlockSpec((None, tn, d), lambda j, i: (layer, j, 0)),
        ],
        out_specs=pl.BlockSpec((tn // LANES, tm, LANES), lambda j, i: (j, i, 0)),
        out_shape=jax.ShapeDtypeStruct((n_main // LANES, t, LANES), BF16),
        scratch_shapes=[pltpu.VMEM((tn, d), BF16)],
        compiler_params=_params(("arbitrary", "arbitrary"), 2 * tm * d * 2 + d * tn * 10 + 3 * tm * tn * 4),
        name="in_proj",
    )(h, w_in_t)


def _forget_kernel(h_ref, w_ref, b_ref, o_ref, carry_ref):
    @pl.when(pl.program_id(1) == 0)
    def _():
        carry_ref[...] = jnp.zeros_like(carry_ref)

    h = h_ref[...]
    w_hi, w_lo = _split2(w_ref[...])
    f = _dot_nt(w_hi, h) + _dot_nt(w_lo, h) + b_ref[...]
    log_f = jnp.minimum(f, 0.0) - jnp.log(1.0 + jnp.exp(-jnp.abs(f)))
    nh, bs = log_f.shape
    row = lax.broadcasted_iota(I32, (bs, bs), 0)
    col = lax.broadcasted_iota(I32, (bs, bs), 1)
    triu = jnp.where(row <= col, 1.0, 0.0).astype(BF16)
    p0 = log_f.astype(BF16)
    r1 = log_f - p0.astype(F32)
    p1 = r1.astype(BF16)
    p2 = (r1 - p1.astype(F32)).astype(BF16)
    cs = _dot(p0, triu) + (_dot(p1, triu) + _dot(p2, triu)) + carry_ref[...]
    carry_ref[...] = cs[:, bs - 1:bs]
    o_ref[...] = cs


def _forget_cumsum(h, w_f_t, b_f, bsz, seq):
    t, d = h.shape
    nh = w_f_t.shape[0]
    bs = _tile(seq, 512)
    per_b = seq // bs
    return pl.pallas_call(
        _forget_kernel,
        grid=(bsz, per_b),
        in_specs=[
            pl.BlockSpec((bs, d), lambda b, i: (b * per_b + i, 0)),
            pl.BlockSpec((nh, d), lambda b, i: (0, 0)),
            pl.BlockSpec((nh, 1), lambda b, i: (0, 0)),
        ],
        out_specs=pl.BlockSpec((None, nh, bs), lambda b, i: (b, 0, i)),
        out_shape=jax.ShapeDtypeStruct((bsz, nh, seq), F32),
        scratch_shapes=[pltpu.VMEM((nh, 1), F32)],
        compiler_params=_params(("arbitrary", "arbitrary"), 2 * bs * d * 2 + nh * d * 16 + bs * bs * 8),
        name="forget_cumsum",
    )(h, w_f_t, b_f.reshape(nh, 1))


def _conv_kernel(bg_ref, cg_ref, hc_ref, w_ref, g_ref, o_ref, ubuf_ref, *, per_b, d_conv):
    i = pl.program_id(0)
    ts = bg_ref.shape[1]
    halo = SUBLANES

    @pl.when(i % per_b == 0)
    def _():
        ubuf_ref[:, 0:halo, :] = jnp.zeros((ubuf_ref.shape[0], halo, LANES), F32)

    @pl.when(i % per_b != 0)
    def _():
        ubuf_ref[:, 0:halo, :] = ubuf_ref[:, ts:ts + halo, :]

    u = cg_ref[...].astype(F32) * hc_ref[...].astype(F32)
    ubuf_ref[:, halo:halo + ts, :] = u
    u1 = ubuf_ref[:, halo - 1:halo - 1 + ts, :]
    u2 = ubuf_ref[:, halo - 2:halo - 2 + ts, :]
    conv = u2 * w_ref[0] + u1 * w_ref[1] + u * w_ref[2]
    y = bg_ref[...].astype(F32) * conv
    ssq = jnp.sum(jnp.sum(y * y, axis=0), axis=-1, keepdims=True)
    r = lax.rsqrt(ssq * (1.0 / d_conv) + LN_EPS)
    o_ref[...] = (y * r[None] * g_ref[...]).astype(BF16)


def _gated_conv(proj, conv_w, g_conv, seq):
    _, t, _ = proj.shape
    width, d_conv = conv_w.shape
    assert width == 3, "causal depthwise convolution of width 3"
    nc = d_conv // LANES
    ts = _tile(seq, 512)
    per_b = seq // ts
    slab = lambda k: pl.BlockSpec((nc, ts, LANES), lambda i: (k, i, 0))
    return pl.pallas_call(
        functools.partial(_conv_kernel, per_b=per_b, d_conv=d_conv),
        grid=(t // ts,),
        in_specs=[
            slab(0), slab(1), slab(2),
            pl.BlockSpec((width, nc, 1, LANES), lambda i: (0, 0, 0, 0)),
            pl.BlockSpec((nc, 1, LANES), lambda i: (0, 0, 0)),
        ],
        out_specs=pl.BlockSpec((nc, ts, LANES), lambda i: (0, i, 0)),
        out_shape=jax.ShapeDtypeStruct((nc, t, LANES), BF16),
        scratch_shapes=[pltpu.VMEM((nc, ts + 2 * SUBLANES, LANES), F32)],
        compiler_params=_params(("arbitrary",), nc * ts * LANES * (2 * 4 * 2 + 4 * 8)),
        name="gated_conv",
    )(proj, proj, proj, conv_w.reshape(width, nc, 1, LANES), g_conv.reshape(nc, 1, LANES))


def _attn_kernel(q_ref, k_ref, v_ref, ck_ref, o_ref, m_ref, l_ref, acc_ref, *, blk):
    qi = pl.program_id(2)
    n_heads = q_ref.shape[0]
    m_ref[...] = jnp.full(m_ref.shape, NEG_INF, F32)
    l_ref[...] = jnp.zeros(l_ref.shape, F32)
    acc_ref[...] = jnp.zeros(acc_ref.shape, F32)
    ones = jnp.ones((blk, LANES), BF16)

    def step(kb, masked):
        ks = pl.multiple_of(kb * blk, blk)
        for g in range(n_heads):
            k = k_ref[g, pl.ds(ks, blk), :]
            v = v_ref[g, pl.ds(ks, blk), :]
            t = _dot_nt(q_ref[g], k) - ck_ref[g, kb] * LOG2E
            if masked:
                row = lax.broadcasted_iota(I32, t.shape, 0)
                col = lax.broadcasted_iota(I32, t.shape, 1)
                t = jnp.where(col <= row, t, NEG_INF)
            m = m_ref[g]
            m_new = jnp.maximum(m, jnp.broadcast_to(jnp.max(t, axis=1, keepdims=True), m.shape))
            p = jnp.exp2(t - jnp.tile(m_new, (1, blk // LANES)))
            alpha = jnp.exp2(m - m_new)
            pv = _dot(p.astype(BF16), jnp.concatenate([v, ones], axis=1))
            l_ref[g] = alpha * l_ref[g] + pv[:, HEAD_DIM:]
            acc_ref[g] = alpha * acc_ref[g] + pv[:, :HEAD_DIM]
            m_ref[g] = m_new

    def body(kb, carry):
        step(kb, False)
        return carry

    lax.fori_loop(0, qi, body, 0)
    step(qi, True)
    for g in range(n_heads):
        o_ref[g] = (acc_ref[g] / l_ref[g]).astype(BF16)


def _attention(proj, cum, bsz, seq, q_slab, n_heads):
    n_slabs = proj.shape[0]
    blk = _tile(seq, 256)
    nq = seq // blk
    grp = _tile(math.gcd(n_heads, q_slab), 8)
    proj4 = proj.reshape(n_slabs, bsz, seq, LANES)
    ck = cum.reshape(bsz, n_heads, nq, 1, blk)
    kv = lambda off: pl.BlockSpec((grp, None, seq, LANES), lambda b, h, i: (off // grp + h, b, 0, 0))
    out = pl.pallas_call(
        functools.partial(_attn_kernel, blk=blk),
        grid=(bsz, n_heads // grp, nq),
        in_specs=[
            pl.BlockSpec((grp, None, blk, LANES), lambda b, h, i: (q_slab // grp + h, b, i, 0)),
            kv(q_slab + n_heads), kv(q_slab + 2 * n_heads),
            pl.BlockSpec((None, grp, nq, 1, blk), lambda b, h, i: (b, h, 0, 0, 0)),
        ],
        out_specs=pl.BlockSpec((grp, None, blk, LANES), lambda b, h, i: (h, b, i, 0)),
        out_shape=jax.ShapeDtypeStruct((n_heads, bsz, seq, LANES), BF16),
        scratch_shapes=[
            pltpu.VMEM((grp, blk, LANES), F32),
            pltpu.VMEM((grp, blk, LANES), F32),
            pltpu.VMEM((grp, blk, HEAD_DIM), F32),
        ],
        compiler_params=_params(("arbitrary", "arbitrary", "arbitrary"),
                                grp * (4 * seq * LANES * 2 + 2 * seq * 4 * SUBLANES + 4 * blk * LANES * 2
                                       + 2 * blk * LANES * 4 + blk * HEAD_DIM * 4 + 8 * blk * blk * 4)),
        name="fox_attention",
    )(proj4, proj4, proj4, ck)
    return out.reshape(n_heads, bsz * seq, LANES)


def _outproj_kernel(yc_ref, oa_ref, ga_ref, w_ref, o_ref, ybuf_ref, *, d_att):
    nc, na = yc_ref.shape[0], oa_ref.shape[0]

    @pl.when(pl.program_id(1) == 0)
    def _():
        for cb in range(nc):
            ybuf_ref[:, cb * LANES:(cb + 1) * LANES] = yc_ref[cb]
        oa = oa_ref[...].astype(F32)
        ssq = jnp.sum(jnp.sum(oa * oa, axis=0), axis=-1, keepdims=True)
        r = lax.rsqrt(ssq * (1.0 / d_att) + LN_EPS)
        for cb in range(na):
            ybuf_ref[:, (nc + cb) * LANES:(nc + cb + 1) * LANES] = (oa[cb] * r * ga_ref[cb]).astype(BF16)

    o_ref[...] = _dot(ybuf_ref[...], w_ref[...].astype(BF16)).astype(BF16)


def _outproj(y_conv, o_att, g_attn, w_out, layer):
    nc, t, _ = y_conv.shape
    na = o_att.shape[0]
    d_mix, d = w_out.shape[1:]
    tm = _tile(t, 1024)
    tn = _tile(d, 512)
    return pl.pallas_call(
        functools.partial(_outproj_kernel, d_att=na * LANES),
        grid=(t // tm, d // tn),
        in_specs=[
            pl.BlockSpec((nc, tm, LANES), lambda i, j: (0, i, 0)),
            pl.BlockSpec((na, tm, LANES), lambda i, j: (0, i, 0)),
            pl.BlockSpec((na, 1, LANES), lambda i, j: (0, 0, 0)),
            pl.BlockSpec((None, d_mix, tn), lambda i, j: (layer, 0, j)),
        ],
        out_specs=pl.BlockSpec((tm, tn), lambda i, j: (i, j)),
        out_shape=jax.ShapeDtypeStruct((t, d), BF16),
        scratch_shapes=[pltpu.VMEM((tm, d_mix), BF16)],
        compiler_params=_params(("arbitrary", "arbitrary"),
                                2 * tm * d_mix * 2 + tm * d_mix * 2 + d_mix * tn * 10 + 3 * tm * tn * 4),
        name="out_proj",
    )(y_conv, o_att, g_attn.reshape(na, 1, LANES), w_out)


def _ln_finish(z, g_ref, b_ref):
    mu = jnp.mean(z, axis=-1, keepdims=True)
    zc = z - mu
    var = jnp.mean(zc * zc, axis=-1, keepdims=True)
    return zc * lax.rsqrt(var + LN_EPS) * g_ref[...] + b_ref[...]


def _ln_mixer_kernel(x_ref, y_ref, gt_ref, g_ref, b_ref, sc_ref, sh_ref, wr_ref, br_ref,
                     xo_ref, hp_ref, lg_ref, *, alpha):
    xn = _ln_finish(alpha * x_ref[...] + (1.0 + gt_ref[...]) * y_ref[...].astype(F32), g_ref, b_ref)
    h = xn * (1.0 + sc_ref[...]) + sh_ref[...]
    half = h.shape[1] // 2
    xo_ref[...] = xn
    _store_slabs(hp_ref, _pack_bf16_pair(h[:, :half], h[:, half:]))
    lg_ref[...] = _dot_f32x3(h, wr_ref[...]) + br_ref[...]


def _ln_moe_kernel(*refs, alpha, with_h):
    x_ref, y0_ref, y1_ref, wt_ref, gt_ref, g_ref, b_ref = refs[:7]
    wt = wt_ref[...]
    w0, w1 = wt[:, 0:1], wt[:, 1:2]
    tm = x_ref.shape[0]
    a0, b0 = _unpack_bf16_pair(_load_slabs(y0_ref, tm))
    a1, b1 = _unpack_bf16_pair(_load_slabs(y1_ref, tm))
    y = jnp.concatenate([w0 * a0 + w1 * a1, w0 * b0 + w1 * b1], axis=1)
    xn = _ln_finish(alpha * x_ref[...] + (1.0 + gt_ref[...]) * y, g_ref, b_ref)
    if with_h:
        sc_ref, sh_ref, xo_ref, ho_ref = refs[7:]
        ho_ref[...] = (xn * (1.0 + sc_ref[...]) + sh_ref[...]).astype(BF16)
    else:
        (xo_ref,) = refs[7:]
    xo_ref[...] = xn


def _ln_specs(t, d, seq, tm):
    per_b = seq // tm
    tok = pl.BlockSpec((tm, d), lambda i: (i, 0))
    vec = pl.BlockSpec((None, 1, d), lambda i: (i // per_b, 0, 0))
    row = pl.BlockSpec((1, d), lambda i: (0, 0))
    return tok, vec, row


def _ln_after_mixer(x2, y, gt, g, b, sc, sh, w_router, b_router, seq, alpha):
    t, d = x2.shape
    tm = _tile(seq, 256)
    tok, vec, row = _ln_specs(t, d, seq, tm)
    r = d // 2 // LANES
    return pl.pallas_call(
        functools.partial(_ln_mixer_kernel, alpha=alpha),
        grid=(t // tm,),
        in_specs=[tok, tok, vec, row, row, vec, vec,
                  pl.BlockSpec((d, LANES), lambda i: (0, 0)), pl.BlockSpec((1, LANES), lambda i: (0, 0))],
        out_specs=[tok, pl.BlockSpec((tm * r, LANES), lambda i: (i, 0)), pl.BlockSpec((tm, LANES), lambda i: (i, 0))],
        out_shape=[jax.ShapeDtypeStruct((t, d), F32), jax.ShapeDtypeStruct((t * r, LANES), U32),
                   jax.ShapeDtypeStruct((t, LANES), F32)],
        compiler_params=_params(("arbitrary",), tm * d * 44 + d * LANES * 16),
        name="ln_mixer",
    )(x2, y, gt, g.reshape(1, d), b.reshape(1, d), sc, sh, w_router, b_router)


def _ln_after_moe(x2, yk, wts, gt, g, b, seq, alpha, nxt=None):
    t, d = x2.shape
    tm = _tile(seq, 256)
    tok, vec, row = _ln_specs(t, d, seq, tm)
    r = yk.shape[1] // t
    yk_spec = lambda k: pl.BlockSpec((None, tm * r, LANES), lambda i: (k, i, 0))
    ins = [x2, yk, yk, wts, gt, g.reshape(1, d), b.reshape(1, d)]
    in_specs = [tok, yk_spec(0), yk_spec(1), pl.BlockSpec((tm, LANES), lambda i: (i, 0)), vec, row, row]
    outs, out_specs = [jax.ShapeDtypeStruct((t, d), F32)], [tok]
    if nxt is not None:
        ins += list(nxt)
        in_specs += [vec, vec]
        outs.append(jax.ShapeDtypeStruct((t, d), BF16))
        out_specs.append(tok)
    return pl.pallas_call(
        functools.partial(_ln_moe_kernel, alpha=alpha, with_h=nxt is not None),
        grid=(t // tm,),
        in_specs=in_specs,
        out_specs=out_specs,
        out_shape=outs,
        compiler_params=_params(("arbitrary",), tm * d * 44),
        name="ln_moe",
    )(*ins)


def _route_kernel(lg_ref, meta_ref, wts_ref, cnt_ref, carry_ref, *, n_groups, per_group):
    @pl.when(pl.program_id(0) == 0)
    def _():
        carry_ref[...] = jnp.zeros_like(carry_ref)

    lg = lg_ref[...]
    tb = lg.shape[0]
    lane = lax.broadcasted_iota(I32, lg.shape, 1)
    n_exp = n_groups * per_group

    def first_argmax(vals, vmax):
        return jnp.min(jnp.where(vals == vmax, lane, LANES), axis=1, keepdims=True)

    gl = jnp.where(lane < n_groups, lg, NEG_INF)
    g_max = jnp.max(gl, axis=1, keepdims=True)
    g_val = 1.0 / jnp.sum(jnp.exp(gl - g_max), axis=1, keepdims=True)
    g_idx = first_argmax(gl, g_max)
    lo = n_groups + g_idx * per_group
    el = jnp.where((lane >= lo) & (lane < lo + per_group) & (lane < n_groups + n_exp), lg, NEG_INF)
    e1 = jnp.max(el, axis=1, keepdims=True)
    i1 = first_argmax(el, e1)
    el2 = jnp.where(lane == i1, NEG_INF, el)
    e2 = jnp.max(el2, axis=1, keepdims=True)
    i2 = first_argmax(el2, e2)
    d = jnp.exp(e2 - e1)
    w1 = g_val / (1.0 + d)
    w2 = g_val * d / (1.0 + d)
    wts_ref[...] = jnp.where(lane == 0, w1, jnp.where(lane == 1, w2, 0.0))

    sel = jnp.where((lane == i1) | (lane == i2), 1.0, 0.0)
    row = lax.broadcasted_iota(I32, (tb, tb), 0)
    col = lax.broadcasted_iota(I32, (tb, tb), 1)
    strict_lower = jnp.where(col < row, 1.0, 0.0).astype(BF16)
    rank = _dot(strict_lower, sel.astype(BF16)) + carry_ref[...]
    pos1 = jnp.sum(jnp.where(lane == i1, rank, 0.0), axis=1, keepdims=True).astype(I32)
    pos2 = jnp.sum(jnp.where(lane == i2, rank, 0.0), axis=1, keepdims=True).astype(I32)
    total = rank[tb - 1:tb, :] + sel[tb - 1:tb, :]
    carry_ref[...] = total
    cnt_ref[...] = total.astype(I32)
    meta = jnp.where(lane == 0, i1 - n_groups, jnp.where(lane == 1, i2 - n_groups,
                     jnp.where(lane == 2, pos1, jnp.where(lane == 3, pos2, 0))))
    meta_ref[...] = meta.T[:SUBLANES, :]


def _route(logits, n_groups, per_group):
    t = logits.shape[0]
    tb = _tile(t, 512)
    blk = pl.BlockSpec((tb, LANES), lambda i: (i, 0))
    one = pl.BlockSpec((1, LANES), lambda i: (0, 0))
    return pl.pallas_call(
        functools.partial(_route_kernel, n_groups=n_groups, per_group=per_group),
        grid=(t // tb,),
        in_specs=[blk],
        out_specs=[pl.BlockSpec((SUBLANES, tb), lambda i: (0, i)), blk, one],
        out_shape=[jax.ShapeDtypeStruct((SUBLANES, t), I32), jax.ShapeDtypeStruct((t, LANES), F32),
                   jax.ShapeDtypeStruct((1, LANES), I32)],
        scratch_shapes=[pltpu.VMEM((1, LANES), F32)],
        compiler_params=_params(("arbitrary",), tb * LANES * 4 * 24 + tb * tb * 8),
        name="route",
    )(logits)


SCALAR_UNROLL = 8


def _gather_rows_kernel(idx_ref, trips_ref, src_ref, dst_ref, vbuf_ref, row_ref, gsem, osem, *, tile, r, fanout):
    n_tiles = trips_ref[2]
    if fanout:
        def clear(i, c):
            for u in range(SCALAR_UNROLL):
                row_ref[i * SCALAR_UNROLL + u] = 0
            return c

        def put(i, c):
            base = i * SCALAR_UNROLL
            dst = [idx_ref[base + u] for u in range(SCALAR_UNROLL)]
            for u in range(SCALAR_UNROLL):
                row_ref[dst[u]] = lax.div(base + u, fanout)
            return c

        lax.fori_loop(0, lax.div(trips_ref[0], SCALAR_UNROLL), clear, 0)
        lax.fori_loop(0, lax.div(trips_ref[1], SCALAR_UNROLL), put, 0)
        rows = row_ref
    else:
        rows = idx_ref

    def gather_start(ti, slot):
        def group(i, c):
            base = i * SCALAR_UNROLL
            srcs = [rows[ti * tile + base + u] for u in range(SCALAR_UNROLL)]
            for u in range(SCALAR_UNROLL):
                s = pl.multiple_of(srcs[u] * r, r)
                d = pl.multiple_of((base + u) * r, r)
                pltpu.make_async_copy(src_ref.at[pl.ds(s, r)], vbuf_ref.at[slot, pl.ds(d, r)],
                                      gsem.at[slot]).start(priority=u % 2)
            return c

        lax.fori_loop(0, tile // SCALAR_UNROLL, group, 0)

    def gather_wait(slot):
        pltpu.make_async_copy(src_ref.at[pl.ds(0, tile * r)], vbuf_ref.at[slot], gsem.at[slot]).wait()

    def tile_out(ti, slot):
        start = pl.multiple_of(ti * (tile * r), tile * r)
        return pltpu.make_async_copy(vbuf_ref.at[slot], dst_ref.at[pl.ds(start, tile * r)], osem.at[slot])

    gather_start(0, 0)

    def step(ti, c):
        slot = ti % 2

        @pl.when(ti >= 1)
        def _():
            tile_out(ti - 1, 1 - slot).wait()

        @pl.when(ti + 1 < n_tiles)
        def _():
            gather_start(ti + 1, 1 - slot)

        gather_wait(slot)
        tile_out(ti, slot).start()
        return c

    lax.fori_loop(0, n_tiles, step, 0)
    tile_out(n_tiles - 1, (n_tiles - 1) % 2).wait()


GATHER_TILE = 256


def _gather_rows(src, r, idx, n_dst, fanout=0):
    tile = _tile(n_dst, GATHER_TILE)
    assert src.shape[0] >= tile * r and tile % SCALAR_UNROLL == 0 and idx.shape[0] % SCALAR_UNROLL == 0
    trips = jnp.array([n_dst, idx.shape[0], n_dst // tile], I32)
    return pl.pallas_call(
        functools.partial(_gather_rows_kernel, tile=tile, r=r, fanout=fanout),
        grid_spec=pltpu.PrefetchScalarGridSpec(
            num_scalar_prefetch=2,
            grid=(1,),
            in_specs=[pl.BlockSpec(memory_space=pl.ANY)],
            out_specs=pl.BlockSpec(memory_space=pl.ANY),
            scratch_shapes=[pltpu.VMEM((2, tile * r, LANES), src.dtype),
                            pltpu.SMEM((n_dst if fanout else 1,), I32),
                            pltpu.SemaphoreType.DMA((2,)), pltpu.SemaphoreType.DMA((2,))],
        ),
        out_shape=jax.ShapeDtypeStruct((n_dst * r, LANES), src.dtype),
        compiler_params=_params(("arbitrary",), 2 * tile * r * LANES * 4),
        name="gather_rows",
    )(idx, trips, src)


CAST_ROWS = 512


def _expert_weight_copies(w_hbm, stage, sems, layer, e):
    return [pltpu.make_async_copy(w.at[layer, e], s, sems.at[k]) for k, (w, s) in enumerate(zip(w_hbm, stage))]


def _expert_ffn_kernel(te_ref, first_ref, next_ref, nt_ref, x_ref, wg_hbm, wu_hbm, wd_hbm, o_ref,
                       wg_st, wu_st, wd_st, wg_bf, wu_bf, wd_bf, sems, *, layer, tm):
    i = pl.program_id(0)
    stage = (wg_st, wu_st, wd_st)
    copies = lambda e: _expert_weight_copies((wg_hbm, wu_hbm, wd_hbm), stage, sems, layer, e)

    @pl.when(i == 0)
    def _():
        for cp in copies(te_ref[0]):
            cp.start()

    @pl.when((i < nt_ref[0]) & (first_ref[i] == 1))
    def _():
        for cp in copies(te_ref[i]):
            cp.wait()
        for st, bf in zip(stage, (wg_bf, wu_bf, wd_bf)):
            rows = st.shape[0]
            for r0 in range(0, rows, CAST_ROWS):
                r1 = min(r0 + CAST_ROWS, rows)
                bf[r0:r1, :] = st[r0:r1, :].astype(BF16)

        @pl.when(next_ref[i] >= 0)
        def _():
            for cp in copies(next_ref[i]):
                cp.start()

    @pl.when(i < nt_ref[0])
    def _():
        xa, xb = _unpack_bf16_pair(_load_slabs(x_ref, tm))
        xa, xb = xa.astype(BF16), xb.astype(BF16)
        half = xa.shape[1]
        a = _dot(xa, wg_bf[:half, :]) + _dot(xb, wg_bf[half:, :])
        b = _dot(xa, wu_bf[:half, :]) + _dot(xb, wu_bf[half:, :])
        hid = (a * _sigmoid(a) * b).astype(BF16)
        ya = _dot(hid, wd_bf[:, :half])
        yb = _dot(hid, wd_bf[:, half:])
        _store_slabs(o_ref, _pack_bf16_pair(ya, yb))

    @pl.when(i >= nt_ref[0])
    def _():
        o_ref[...] = jnp.zeros(o_ref.shape, o_ref.dtype)


def _expert_ffn(xs, r, tile_expert, tile_first, tile_next, n_tiles, w_gate, w_up, w_down, layer, tm):
    p = xs.shape[0] // r
    d, f = w_gate.shape[2:]
    hbm = pl.BlockSpec(memory_space=pl.ANY)
    tile = pl.BlockSpec((tm * r, LANES), lambda i, *_: (i, 0))
    return pl.pallas_call(
        functools.partial(_expert_ffn_kernel, layer=layer, tm=tm),
        grid_spec=pltpu.PrefetchScalarGridSpec(
            num_scalar_prefetch=4,
            grid=(p // tm,),
            in_specs=[tile, hbm, hbm, hbm],
            out_specs=tile,
            scratch_shapes=[pltpu.VMEM((d, f), F32), pltpu.VMEM((d, f), F32), pltpu.VMEM((f, d), F32),
                            pltpu.VMEM((d, f), BF16), pltpu.VMEM((d, f), BF16), pltpu.VMEM((f, d), BF16),
                            pltpu.SemaphoreType.DMA((3,))],
        ),
        out_shape=jax.ShapeDtypeStruct((p * r, LANES), U32),
        compiler_params=_params(("arbitrary",), 3 * d * f * 6 + 4 * tm * r * LANES * 4 + 6 * tm * d * 4),
        name="expert_ffn",
    )(tile_expert, tile_first, tile_next, n_tiles, xs, w_gate, w_up, w_down)


def _moe_sparse(hp, r, logits, w_gate, w_up, w_down, layer, n_groups, per_group):
    t = hp.shape[0] // r
    n_exp = n_groups * per_group
    tm = 256 if t * TOP_K >= 256 * n_exp else 8
    meta, wts, cnt = _route(logits, n_groups, per_group)

    counts = cnt[0, n_groups:n_groups + n_exp]
    tiles_e = (counts + tm - 1) // tm
    tile_end = jnp.cumsum(tiles_e)
    base = (tile_end - tiles_e) * tm
    n_tiles = tile_end[-1]
    max_tiles = (t * TOP_K) // tm + n_exp
    all_tiles = jnp.arange(max_tiles, dtype=I32)
    tile_expert = jnp.searchsorted(tile_end, jnp.minimum(all_tiles, n_tiles - 1), side="right").astype(I32)
    slot = (base[meta[0:TOP_K]] + meta[TOP_K:2 * TOP_K]).astype(I32)

    prev_expert = jnp.concatenate([jnp.full((1,), -1, I32), tile_expert[:-1]])
    tile_first = ((all_tiles < n_tiles) & (tile_expert != prev_expert)).astype(I32)
    group_end = tile_end[tile_expert]
    tile_next = jnp.where(group_end < n_tiles, tile_expert[jnp.minimum(group_end, max_tiles - 1)], -1).astype(I32)

    n_tiles = n_tiles.reshape(1).astype(I32)
    xs = _gather_rows(hp, r, slot.T.reshape(-1), max_tiles * tm, fanout=TOP_K)
    ys = _expert_ffn(xs, r, tile_expert, tile_first, tile_next, n_tiles, w_gate, w_up, w_down, layer, tm)
    yk = _gather_rows(ys, r, slot.reshape(-1), TOP_K * t)
    return yk.reshape(TOP_K, t * r, LANES), wts


def kernel(x, c, w_cond, w_mod, b_mod, w_in, b_forget, conv_w, g_conv_out, g_attn_out, w_out, ln1_g, ln1_b, w_router_group, b_router_group, w_router_expert, b_router_expert, w_gate, w_up, w_down, ln2_g, ln2_b):
    bsz, seq, d = x.shape
    depth = w_mod.shape[0]
    d_conv = conv_w.shape[2]
    n_heads = b_forget.shape[1]
    d_att = n_heads * HEAD_DIM
    n_main = 3 * d_conv + 3 * d_att
    n_groups = w_router_group.shape[2]
    n_exp = w_router_expert.shape[2]
    alpha = (2 * depth) ** 0.25
    assert w_in.shape[2] == n_main + n_heads and n_groups + n_exp <= LANES
    assert d_conv % LANES == 0 and (d // 2) % LANES == 0

    mod = _modulation(c, w_cond, w_mod, b_mod).reshape(depth, bsz, 6, 1, d)
    part = lambda l, k: mod[l, :, k]
    pad = LANES - n_groups - n_exp
    w_router = jnp.pad(jnp.concatenate([w_router_group, w_router_expert], axis=2), ((0, 0), (0, 0), (0, pad)))
    b_router = jnp.pad(jnp.concatenate([b_router_group, b_router_expert], axis=1), ((0, 0), (0, pad)))
    w_in_t = jnp.swapaxes(w_in, 1, 2)
    q_cols = (3 * d_conv, 3 * d_conv + d_att)
    q_scale = HEAD_DIM ** -0.5 * LOG2E

    x2 = x.reshape(bsz * seq, d)
    h = _modulate(x2, part(0, 1), part(0, 0), seq)
    for l in range(depth):
        proj = _inproj(h, w_in_t, l, n_main, q_cols, q_scale)
        cum = _forget_cumsum(h, w_in_t[l, n_main:], b_forget[l], bsz, seq)
        y_conv = _gated_conv(proj, conv_w[l], g_conv_out[l], seq)
        o_att = _attention(proj, cum, bsz, seq, 3 * d_conv // LANES, n_heads)
        y = _outproj(y_conv, o_att, g_attn_out[l], w_out, l)
        x2, hp, logits = _ln_after_mixer(x2, y, part(l, 2), ln1_g[l], ln1_b[l], part(l, 4), part(l, 3),
                                         w_router[l], b_router[l].reshape(1, LANES), seq, alpha)
        yk, wts = _moe_sparse(hp, d // 2 // LANES, logits, w_gate, w_up, w_down, l, n_groups, n_exp // n_groups)
        if l + 1 < depth:
            x2, h = _ln_after_moe(x2, yk, wts, part(l, 5), ln2_g[l], ln2_b[l], seq, alpha,
                                  nxt=(part(l + 1, 1), part(l + 1, 0)))
        else:
            (x2,) = _ln_after_moe(x2, yk, wts, part(l, 5), ln2_g[l], ln2_b[l], seq, alpha)
    return x2.reshape(bsz, seq, d)
```

```python
import functools
import math

import jax
import jax.numpy as jnp
from jax import lax
from jax.experimental import pallas as pl
from jax.experimental.pallas import tpu as pltpu

F32 = jnp.float32
BF16 = jnp.bfloat16
U32 = jnp.uint32
I32 = jnp.int32

LANES = 128
SUBLANES = 8
VMEM_BYTES = 64 * 1024 * 1024
HEAD_DIM = 128
TOP_K = 2
LN_EPS = 1e-5
LOG2E = math.log2(math.e)
NEG_INF = float("-inf")
HI16 = 0xFFFF0000


def _tile(n, pref):
    t = min(n, pref)
    while n % t:
        t -= 1
    return t


def _params(semantics, vmem_bytes):
    limit = min(VMEM_BYTES - (4 << 20), vmem_bytes + (8 << 20))
    return pltpu.CompilerParams(dimension_semantics=semantics, vmem_limit_bytes=int(limit))


def _split2(a):
    hi = a.astype(BF16)
    lo = (a - hi.astype(F32)).astype(BF16)
    return hi, lo


def _dot(a, b):
    return jnp.dot(a, b, preferred_element_type=F32)


def _dot_nt(a, b):
    return lax.dot_general(a, b, (((1,), (1,)), ((), ())), preferred_element_type=F32)


def _dot_f32x3(a, b):
    ah, al = _split2(a)
    bh, bl = _split2(b)
    return _dot(ah, bh) + (_dot(al, bh) + _dot(ah, bl))


def _sigmoid(x):
    return 1.0 / (1.0 + jnp.exp(-x))


def _pack_bf16_pair(a, b):
    ua = lax.bitcast_convert_type(a.astype(BF16).astype(F32), U32)
    ub = lax.bitcast_convert_type(b.astype(BF16).astype(F32), U32)
    return ua | lax.shift_right_logical(ub, jnp.uint32(16))


def _unpack_bf16_pair(w):
    a = lax.bitcast_convert_type(w & jnp.uint32(HI16), F32)
    b = lax.bitcast_convert_type(lax.shift_left(w, jnp.uint32(16)), F32)
    return a, b


def _store_slabs(ref, val):
    rows, width = val.shape
    r = width // LANES
    for c in range(r):
        ref[pl.ds(c, rows, stride=r), :] = val[:, c * LANES:(c + 1) * LANES]


def _load_slabs(ref, rows):
    r = ref.shape[0] // rows
    return jnp.concatenate([ref[pl.ds(c, rows, stride=r), :] for c in range(r)], axis=1)


def _cond_embed_kernel(c_ref, w_ref, e_ref):
    c = c_ref[...]
    e_ref[...] = _dot_f32x3(c * _sigmoid(c), w_ref[...])


def _mod_kernel(e_ref, w_ref, b_ref, o_ref):
    o_ref[...] = _dot_f32x3(e_ref[...], w_ref[...]) + b_ref[...]


def _modulation(c, w_cond, w_mod, b_mod):
    bsz, d = c.shape
    depth, rank, n_mod = w_mod.shape
    rows = SUBLANES
    c_pad = jnp.pad(c, ((0, rows - bsz), (0, 0)))
    e = pl.pallas_call(
        _cond_embed_kernel,
        out_shape=jax.ShapeDtypeStruct((rows, rank), F32),
        compiler_params=_params(None, 8 * (rows * d + d * rank + rows * rank)),
        name="cond_embed",
    )(c_pad, w_cond)
    tn = _tile(n_mod, 4096)
    mod = pl.pallas_call(
        _mod_kernel,
        grid=(depth, n_mod // tn),
        in_specs=[
            pl.BlockSpec((rows, rank), lambda l, j: (0, 0)),
            pl.BlockSpec((None, rank, tn), lambda l, j: (l, 0, j)),
            pl.BlockSpec((None, 1, tn), lambda l, j: (l, 0, j)),
        ],
        out_specs=pl.BlockSpec((None, rows, tn), lambda l, j: (l, 0, j)),
        out_shape=jax.ShapeDtypeStruct((depth, rows, n_mod), F32),
        compiler_params=_params(("arbitrary", "arbitrary"), 8 * (rank * tn + rows * tn + tn + rows * rank)),
        name="adaln_mod",
    )(e, w_mod, b_mod.reshape(depth, 1, n_mod))
    return mod[:, :bsz]


def _modulate_kernel(x_ref, sc_ref, sh_ref, h_ref):
    h_ref[...] = (x_ref[...] * (1.0 + sc_ref[...]) + sh_ref[...]).astype(BF16)


def _modulate(x2, sc, sh, seq):
    t, d = x2.shape
    tm = _tile(seq, 512)
    per_b = seq // tm
    vec = pl.BlockSpec((None, 1, d), lambda i: (i // per_b, 0, 0))
    return pl.pallas_call(
        _modulate_kernel,
        grid=(t // tm,),
        in_specs=[pl.BlockSpec((tm, d), lambda i: (i, 0)), vec, vec],
        out_specs=pl.BlockSpec((tm, d), lambda i: (i, 0)),
        out_shape=jax.ShapeDtypeStruct((t, d), BF16),
        compiler_params=_params(("arbitrary",), tm * d * 12),
        name="modulate",
    )(x2, sc, sh)


def _inproj_kernel(h_ref, w_ref, o_ref, wbf_ref, *, q_tiles, q_scale):
    @pl.when(pl.program_id(1) == 0)
    def _():
        j = pl.program_id(0)
        scale = jnp.where((j >= q_tiles[0]) & (j < q_tiles[1]), q_scale, 1.0).astype(F32)
        wbf_ref[...] = (w_ref[...] * scale).astype(BF16)

    acc = _dot_nt(h_ref[...], wbf_ref[...])
    for cb in range(o_ref.shape[0]):
        o_ref[cb] = acc[:, cb * LANES:(cb + 1) * LANES].astype(BF16)


def _inproj(h, w_in_t, layer, n_main, q_cols, q_scale):
    t, d = h.shape
    tm = _tile(t, 1024)
    tn = _tile(math.gcd(q_cols[0], q_cols[1] - q_cols[0]), 512)
    q_tiles = (q_cols[0] // tn, q_cols[1] // tn)
    return pl.pallas_call(
        functools.partial(_inproj_kernel, q_tiles=q_tiles, q_scale=q_scale),
        grid=(n_main // tn, t // tm),
        in_specs=[
            pl.BlockSpec((tm, d), lambda j, i: (i, 0)),
            pl.BlockSpec((None, tn, d), lambda j, i: (layer, j, 0)),
        ],
        out_specs=pl.BlockSpec((tn // LANES, tm, LANES), lambda j, i: (j, i, 0)),
        out_shape=jax.ShapeDtypeStruct((n_main // LANES, t, LANES), BF16),
        scratch_shapes=[pltpu.VMEM((tn, d), BF16)],
        compiler_params=_params(("arbitrary", "arbitrary"), 2 * tm * d * 2 + d * tn * 10 + 3 * tm * tn * 4),
        name="in_proj",
    )(h, w_in_t)


def _forget_kernel(h_ref, w_ref, b_ref, o_ref, carry_ref):
    @pl.when(pl.program_id(1) == 0)
    def _():
        carry_ref[...] = jnp.zeros_like(carry_ref)

    h = h_ref[...]
    w_hi, w_lo = _split2(w_ref[...])
    f = _dot_nt(w_hi, h) + _dot_nt(w_lo, h) + b_ref[...]
    log_f = jnp.minimum(f, 0.0) - jnp.log(1.0 + jnp.exp(-jnp.abs(f)))
    nh, bs = log_f.shape
    row = lax.broadcasted_iota(I32, (bs, bs), 0)
    col = lax.broadcasted_iota(I32, (bs, bs), 1)
    triu = jnp.where(row <= col, 1.0, 0.0).astype(BF16)
    p0 = log_f.astype(BF16)
    r1 = log_f - p0.astype(F32)
    p1 = r1.astype(BF16)
    p2 = (r1 - p1.astype(F32)).astype(BF16)
    cs = _dot(p0, triu) + (_dot(p1, triu) + _dot(p2, triu)) + carry_ref[...]
    carry_ref[...] = cs[:, bs - 1:bs]
    o_ref[...] = cs


def _forget_cumsum(h, w_f_t, b_f, bsz, seq):
    t, d = h.shape
    nh = w_f_t.shape[0]
    bs = _tile(seq, 512)
    per_b = seq // bs
    return pl.pallas_call(
        _forget_kernel,
        grid=(bsz, per_b),
        in_specs=[
            pl.BlockSpec((bs, d), lambda b, i: (b * per_b + i, 0)),
            pl.BlockSpec((nh, d), lambda b, i: (0, 0)),
            pl.BlockSpec((nh, 1), lambda b, i: (0, 0)),
        ],
        out_specs=pl.BlockSpec((None, nh, bs), lambda b, i: (b, 0, i)),
        out_shape=jax.ShapeDtypeStruct((bsz, nh, seq), F32),
        scratch_shapes=[pltpu.VMEM((nh, 1), F32)],
        compiler_params=_params(("arbitrary", "arbitrary"), 2 * bs * d * 2 + nh * d * 16 + bs * bs * 8),
        name="forget_cumsum",
    )(h, w_f_t, b_f.reshape(nh, 1))


def _conv_kernel(bg_ref, cg_ref, hc_ref, w_ref, g_ref, o_ref, ubuf_ref, *, per_b, d_conv):
    i = pl.program_id(0)
    ts = bg_ref.shape[1]
    halo = SUBLANES

    @pl.when(i % per_b == 0)
    def _():
        ubuf_ref[:, 0:halo, :] = jnp.zeros((ubuf_ref.shape[0], halo, LANES), F32)

    @pl.when(i % per_b != 0)
    def _():
        ubuf_ref[:, 0:halo, :] = ubuf_ref[:, ts:ts + halo, :]

    u = cg_ref[...].astype(F32) * hc_ref[...].astype(F32)
    ubuf_ref[:, halo:halo + ts, :] = u
    u1 = ubuf_ref[:, halo - 1:halo - 1 + ts, :]
    u2 = ubuf_ref[:, halo - 2:halo - 2 + ts, :]
    conv = u2 * w_ref[0] + u1 * w_ref[1] + u * w_ref[2]
    y = bg_ref[...].astype(F32) * conv
    ssq = jnp.sum(jnp.sum(y * y, axis=0), axis=-1, keepdims=True)
    r = lax.rsqrt(ssq * (1.0 / d_conv) + LN_EPS)
    o_ref[...] = (y * r[None] * g_ref[...]).astype(BF16)


def _gated_conv(proj, conv_w, g_conv, seq):
    _, t, _ = proj.shape
    width, d_conv = conv_w.shape
    assert width == 3, "causal depthwise convolution of width 3"
    nc = d_conv // LANES
    ts = _tile(seq, 512)
    per_b = seq // ts
    slab = lambda k: pl.BlockSpec((nc, ts, LANES), lambda i: (k, i, 0))
    return pl.pallas_call(
        functools.partial(_conv_kernel, per_b=per_b, d_conv=d_conv),
        grid=(t // ts,),
        in_specs=[
            slab(0), slab(1), slab(2),
            pl.BlockSpec((width, nc, 1, LANES), lambda i: (0, 0, 0, 0)),
            pl.BlockSpec((nc, 1, LANES), lambda i: (0, 0, 0)),
        ],
        out_specs=pl.BlockSpec((nc, ts, LANES), lambda i: (0, i, 0)),
        out_shape=jax.ShapeDtypeStruct((nc, t, LANES), BF16),
        scratch_shapes=[pltpu.VMEM((nc, ts + 2 * SUBLANES, LANES), F32)],
        compiler_params=_params(("arbitrary",), nc * ts * LANES * (2 * 4 * 2 + 4 * 8)),
        name="gated_conv",
    )(proj, proj, proj, conv_w.reshape(width, nc, 1, LANES), g_conv.reshape(nc, 1, LANES))


def _attn_kernel(q_ref, k_ref, v_ref, ck_ref, o_ref, m_ref, l_ref, acc_ref, *, blk):
    qi = pl.program_id(2)
    n_heads = q_ref.shape[0]
    m_ref[...] = jnp.full(m_ref.shape, NEG_INF, F32)
    l_ref[...] = jnp.zeros(l_ref.shape, F32)
    acc_ref[...] = jnp.zeros(acc_ref.shape, F32)
    ones = jnp.ones((blk, LANES), BF16)

    def step(kb, masked):
        ks = pl.multiple_of(kb * blk, blk)
        for g in range(n_heads):
            k = k_ref[g, pl.ds(ks, blk), :]
            v = v_ref[g, pl.ds(ks, blk), :]
            t = _dot_nt(q_ref[g], k) - ck_ref[g, kb] * LOG2E
            if masked:
                row = lax.broadcasted_iota(I32, t.shape, 0)
                col = lax.broadcasted_iota(I32, t.shape, 1)
                t = jnp.where(col <= row, t, NEG_INF)
            m = m_ref[g]
            m_new = jnp.maximum(m, jnp.broadcast_to(jnp.max(t, axis=1, keepdims=True), m.shape))
            p = jnp.exp2(t - jnp.tile(m_new, (1, blk // LANES)))
            alpha = jnp.exp2(m - m_new)
            pv = _dot(p.astype(BF16), jnp.concatenate([v, ones], axis=1))
            l_ref[g] = alpha * l_ref[g] + pv[:, HEAD_DIM:]
            acc_ref[g] = alpha * acc_ref[g] + pv[:, :HEAD_DIM]
            m_ref[g] = m_new

    def body(kb, carry):
        step(kb, False)
        return carry

    lax.fori_loop(0, qi, body, 0)
    step(qi, True)
    for g in range(n_heads):
        o_ref[g] = (acc_ref[g] / l_ref[g]).astype(BF16)


def _attention(proj, cum, bsz, seq, q_slab, n_heads):
    n_slabs = proj.shape[0]
    blk = _tile(seq, 256)
    nq = seq // blk
    grp = _tile(math.gcd(n_heads, q_slab), 8)
    proj4 = proj.reshape(n_slabs, bsz, seq, LANES)
    ck = cum.reshape(bsz, n_heads, nq, 1, blk)
    kv = lambda off: pl.BlockSpec((grp, None, seq, LANES), lambda b, h, i: (off // grp + h, b, 0, 0))
    out = pl.pallas_call(
        functools.partial(_attn_kernel, blk=blk),
        grid=(bsz, n_heads // grp, nq),
        in_specs=[
            pl.BlockSpec((grp, None, blk, LANES), lambda b, h, i: (q_slab // grp + h, b, i, 0)),
            kv(q_slab + n_heads), kv(q_slab + 2 * n_heads),
            pl.BlockSpec((None, grp, nq, 1, blk), lambda b, h, i: (b, h, 0, 0, 0)),
        ],
        out_specs=pl.BlockSpec((grp, None, blk, LANES), lambda b, h, i: (h, b, i, 0)),
        out_shape=jax.ShapeDtypeStruct((n_heads, bsz, seq, LANES), BF16),
        scratch_shapes=[
            pltpu.VMEM((grp, blk, LANES), F32),
            pltpu.VMEM((grp, blk, LANES), F32),
            pltpu.VMEM((grp, blk, HEAD_DIM), F32),
        ],
        compiler_params=_params(("arbitrary", "arbitrary", "arbitrary"),
                                grp * (4 * seq * LANES * 2 + 2 * seq * 4 * SUBLANES + 4 * blk * LANES * 2
                                       + 2 * blk * LANES * 4 + blk * HEAD_DIM * 4 + 8 * blk * blk * 4)),
        name="fox_attention",
    )(proj4, proj4, proj4, ck)
    return out.reshape(n_heads, bsz * seq, LANES)


def _outproj_kernel(yc_ref, oa_ref, ga_ref, w_ref, o_ref, ybuf_ref, *, d_att):
    nc, na = yc_ref.shape[0], oa_ref.shape[0]

    @pl.when(pl.program_id(1) == 0)
    def _():
        for cb in range(nc):
            ybuf_ref[:, cb * LANES:(cb + 1) * LANES] = yc_ref[cb]
        oa = oa_ref[...].astype(F32)
        ssq = jnp.sum(jnp.sum(oa * oa, axis=0), axis=-1, keepdims=True)
        r = lax.rsqrt(ssq * (1.0 / d_att) + LN_EPS)
        for cb in range(na):
            ybuf_ref[:, (nc + cb) * LANES:(nc + cb + 1) * LANES] = (oa[cb] * r * ga_ref[cb]).astype(BF16)

    o_ref[...] = _dot(ybuf_ref[...], w_ref[...].astype(BF16)).astype(BF16)


def _outproj(y_conv, o_att, g_attn, w_out, layer):
    nc, t, _ = y_conv.shape
    na = o_att.shape[0]
    d_mix, d = w_out.shape[1:]
    tm = _tile(t, 1024)
    tn = _tile(d, 512)
    return pl.pallas_call(
        functools.partial(_outproj_kernel, d_att=na * LANES),
        grid=(t // tm, d // tn),
        in_specs=[
            pl.BlockSpec((nc, tm, LANES), lambda i, j: (0, i, 0)),
            pl.BlockSpec((na, tm, LANES), lambda i, j: (0, i, 0)),
            pl.BlockSpec((na, 1, LANES), lambda i, j: (0, 0, 0)),
            pl.BlockSpec((None, d_mix, tn), lambda i, j: (layer, 0, j)),
        ],
        out_specs=pl.BlockSpec((tm, tn), lambda i, j: (i, j)),
        out_shape=jax.ShapeDtypeStruct((t, d), BF16),
        scratch_shapes=[pltpu.VMEM((tm, d_mix), BF16)],
        compiler_params=_params(("arbitrary", "arbitrary"),
                                2 * tm * d_mix * 2 + tm * d_mix * 2 + d_mix * tn * 10 + 3 * tm * tn * 4),
        name="out_proj",
    )(y_conv, o_att, g_attn.reshape(na, 1, LANES), w_out)


def _ln_finish(z, g_ref, b_ref):
    mu = jnp.mean(z, axis=-1, keepdims=True)
    zc = z - mu
    var = jnp.mean(zc * zc, axis=-1, keepdims=True)
    return zc * lax.rsqrt(var + LN_EPS) * g_ref[...] + b_ref[...]


def _ln_mixer_kernel(x_ref, y_ref, gt_ref, g_ref, b_ref, sc_ref, sh_ref, wr_ref, br_ref,
                     xo_ref, hp_ref, lg_ref, *, alpha):
    xn = _ln_finish(alpha * x_ref[...] + (1.0 + gt_ref[...]) * y_ref[...].astype(F32), g_ref, b_ref)
    h = xn * (1.0 + sc_ref[...]) + sh_ref[...]
    half = h.shape[1] // 2
    xo_ref[...] = xn
    _store_slabs(hp_ref, _pack_bf16_pair(h[:, :half], h[:, half:]))
    lg_ref[...] = _dot_f32x3(h, wr_ref[...]) + br_ref[...]


def _ln_moe_kernel(*refs, alpha, with_h):
    x_ref, y0_ref, y1_ref, wt_ref, gt_ref, g_ref, b_ref = refs[:7]
    wt = wt_ref[...]
    w0, w1 = wt[:, 0:1], wt[:, 1:2]
    tm = x_ref.shape[0]
    a0, b0 = _unpack_bf16_pair(_load_slabs(y0_ref, tm))
    a1, b1 = _unpack_bf16_pair(_load_slabs(y1_ref, tm))
    y = jnp.concatenate([w0 * a0 + w1 * a1, w0 * b0 + w1 * b1], axis=1)
    xn = _ln_finish(alpha * x_ref[...] + (1.0 + gt_ref[...]) * y, g_ref, b_ref)
    if with_h:
        sc_ref, sh_ref, xo_ref, ho_ref = refs[7:]
        ho_ref[...] = (xn * (1.0 + sc_ref[...]) + sh_ref[...]).astype(BF16)
    else:
        (xo_ref,) = refs[7:]
    xo_ref[...] = xn


def _ln_specs(t, d, seq, tm):
    per_b = seq // tm
    tok = pl.BlockSpec((tm, d), lambda i: (i, 0))
    vec = pl.BlockSpec((None, 1, d), lambda i: (i // per_b, 0, 0))
    row = pl.BlockSpec((1, d), lambda i: (0, 0))
    return tok, vec, row


def _ln_after_mixer(x2, y, gt, g, b, sc, sh, w_router, b_router, seq, alpha):
    t, d = x2.shape
    tm = _tile(seq, 256)
    tok, vec, row = _ln_specs(t, d, seq, tm)
    r = d // 2 // LANES
    return pl.pallas_call(
        functools.partial(_ln_mixer_kernel, alpha=alpha),
        grid=(t // tm,),
        in_specs=[tok, tok, vec, row, row, vec, vec,
                  pl.BlockSpec((d, LANES), lambda i: (0, 0)), pl.BlockSpec((1, LANES), lambda i: (0, 0))],
        out_specs=[tok, pl.BlockSpec((tm * r, LANES), lambda i: (i, 0)), pl.BlockSpec((tm, LANES), lambda i: (i, 0))],
        out_shape=[jax.ShapeDtypeStruct((t, d), F32), jax.ShapeDtypeStruct((t * r, LANES), U32),
                   jax.ShapeDtypeStruct((t, LANES), F32)],
        compiler_params=_params(("arbitrary",), tm * d * 44 + d * LANES * 16),
        name="ln_mixer",
    )(x2, y, gt, g.reshape(1, d), b.reshape(1, d), sc, sh, w_router, b_router)


def _ln_after_moe(x2, yk, wts, gt, g, b, seq, alpha, nxt=None):
    t, d = x2.shape
    tm = _tile(seq, 256)
    tok, vec, row = _ln_specs(t, d, seq, tm)
    r = yk.shape[1] // t
    yk_spec = lambda k: pl.BlockSpec((None, tm * r, LANES), lambda i: (k, i, 0))
    ins = [x2, yk, yk, wts, gt, g.reshape(1, d), b.reshape(1, d)]
    in_specs = [tok, yk_spec(0), yk_spec(1), pl.BlockSpec((tm, LANES), lambda i: (i, 0)), vec, row, row]
    outs, out_specs = [jax.ShapeDtypeStruct((t, d), F32)], [tok]
    if nxt is not None:
        ins += list(nxt)
        in_specs += [vec, vec]
        outs.append(jax.ShapeDtypeStruct((t, d), BF16))
        out_specs.append(tok)
    return pl.pallas_call(
        functools.partial(_ln_moe_kernel, alpha=alpha, with_h=nxt is not None),
        grid=(t // tm,),
        in_specs=in_specs,
        out_specs=out_specs,
        out_shape=outs,
        compiler_params=_params(("arbitrary",), tm * d * 44),
        name="ln_moe",
    )(*ins)


def _route_kernel(lg_ref, meta_ref, wts_ref, cnt_ref, carry_ref, *, n_groups, per_group):
    @pl.when(pl.program_id(0) == 0)
    def _():
        carry_ref[...] = jnp.zeros_like(carry_ref)

    lg = lg_ref[...]
    tb = lg.shape[0]
    lane = lax.broadcasted_iota(I32, lg.shape, 1)
    n_exp = n_groups * per_group

    def first_argmax(vals, vmax):
        return jnp.min(jnp.where(vals == vmax, lane, LANES), axis=1, keepdims=True)

    gl = jnp.where(lane < n_groups, lg, NEG_INF)
    g_max = jnp.max(gl, axis=1, keepdims=True)
    g_val = 1.0 / jnp.sum(jnp.exp(gl - g_max), axis=1, keepdims=True)
    g_idx = first_argmax(gl, g_max)
    lo = n_groups + g_idx * per_group
    el = jnp.where((lane >= lo) & (lane < lo + per_group) & (lane < n_groups + n_exp), lg, NEG_INF)
    e1 = jnp.max(el, axis=1, keepdims=True)
    i1 = first_argmax(el, e1)
    el2 = jnp.where(lane == i1, NEG_INF, el)
    e2 = jnp.max(el2, axis=1, keepdims=True)
    i2 = first_argmax(el2, e2)
    d = jnp.exp(e2 - e1)
    w1 = g_val / (1.0 + d)
    w2 = g_val * d / (1.0 + d)
    wts_ref[...] = jnp.where(lane == 0, w1, jnp.where(lane == 1, w2, 0.0))

    sel = jnp.where((lane == i1) | (lane == i2), 1.0, 0.0)
    row = lax.broadcasted_iota(I32, (tb, tb), 0)
    col = lax.broadcasted_iota(I32, (tb, tb), 1)
    strict_lower = jnp.where(col < row, 1.0, 0.0).astype(BF16)
    rank = _dot(strict_lower, sel.astype(BF16)) + carry_ref[...]
    pos1 = jnp.sum(jnp.where(lane == i1, rank, 0.0), axis=1, keepdims=True).astype(I32)
    pos2 = jnp.sum(jnp.where(lane == i2, rank, 0.0), axis=1, keepdims=True).astype(I32)
    total = rank[tb - 1:tb, :] + sel[tb - 1:tb, :]
    carry_ref[...] = total
    cnt_ref[...] = total.astype(I32)
    meta = jnp.where(lane == 0, i1 - n_groups, jnp.where(lane == 1, i2 - n_groups,
                     jnp.where(lane == 2, pos1, jnp.where(lane == 3, pos2, 0))))
    meta_ref[...] = meta.T[:SUBLANES, :]


def _route(logits, n_groups, per_group):
    t = logits.shape[0]
    tb = _tile(t, 512)
    blk = pl.BlockSpec((tb, LANES), lambda i: (i, 0))
    one = pl.BlockSpec((1, LANES), lambda i: (0, 0))
    return pl.pallas_call(
        functools.partial(_route_kernel, n_groups=n_groups, per_group=per_group),
        grid=(t // tb,),
        in_specs=[blk],
        out_specs=[pl.BlockSpec((SUBLANES, tb), lambda i: (0, i)), blk, one],
        out_shape=[jax.ShapeDtypeStruct((SUBLANES, t), I32), jax.ShapeDtypeStruct((t, LANES), F32),
                   jax.ShapeDtypeStruct((1, LANES), I32)],
        scratch_shapes=[pltpu.VMEM((1, LANES), F32)],
        compiler_params=_params(("arbitrary",), tb * LANES * 4 * 24 + tb * tb * 8),
        name="route",
    )(logits)


SCALAR_UNROLL = 8


def _gather_rows_kernel(idx_ref, trips_ref, src_ref, dst_ref, vbuf_ref, row_ref, gsem, osem,
                        *, tile, r, fanout, n_src):
    n_tiles = trips_ref[2]
    if fanout:
        def clear(i, c):
            for u in range(SCALAR_UNROLL):
                row_ref[i * SCALAR_UNROLL + u] = lax.rem(i * SCALAR_UNROLL + u, n_src)
            return c

        def put(i, c):
            base = i * SCALAR_UNROLL
            dst = [idx_ref[base + u] for u in range(SCALAR_UNROLL)]
            for u in range(SCALAR_UNROLL):
                row_ref[dst[u]] = lax.div(base + u, fanout)
            return c

        lax.fori_loop(0, lax.div(trips_ref[0], SCALAR_UNROLL), clear, 0)
        lax.fori_loop(0, lax.div(trips_ref[1], SCALAR_UNROLL), put, 0)
        rows = row_ref
    else:
        rows = idx_ref

    def gather_start(ti, slot):
        def group(i, c):
            base = i * SCALAR_UNROLL
            srcs = [rows[ti * tile + base + u] for u in range(SCALAR_UNROLL)]
            for u in range(SCALAR_UNROLL):
                s = pl.multiple_of(srcs[u] * r, r)
                d = pl.multiple_of((base + u) * r, r)
                pltpu.make_async_copy(src_ref.at[pl.ds(s, r)], vbuf_ref.at[slot, pl.ds(d, r)],
                                      gsem.at[slot]).start(priority=u % 2)
            return c

        lax.fori_loop(0, tile // SCALAR_UNROLL, group, 0)

    def gather_wait(slot):
        pltpu.make_async_copy(src_ref.at[pl.ds(0, tile * r)], vbuf_ref.at[slot], gsem.at[slot]).wait()

    def tile_out(ti, slot):
        start = pl.multiple_of(ti * (tile * r), tile * r)
        return pltpu.make_async_copy(vbuf_ref.at[slot], dst_ref.at[pl.ds(start, tile * r)], osem.at[slot])

    gather_start(0, 0)

    def step(ti, c):
        slot = ti % 2

        @pl.when(ti >= 1)
        def _():
            tile_out(ti - 1, 1 - slot).wait()

        @pl.when(ti + 1 < n_tiles)
        def _():
            gather_start(ti + 1, 1 - slot)

        gather_wait(slot)
        tile_out(ti, slot).start()
        return c

    lax.fori_loop(0, n_tiles, step, 0)
    tile_out(n_tiles - 1, (n_tiles - 1) % 2).wait()


GATHER_TILE = 256


def _gather_rows(src, r, idx, n_dst, fanout=0):
    tile = _tile(n_dst, GATHER_TILE)
    assert src.shape[0] >= tile * r and tile % SCALAR_UNROLL == 0 and idx.shape[0] % SCALAR_UNROLL == 0
    trips = jnp.array([n_dst, idx.shape[0], n_dst // tile], I32)
    return pl.pallas_call(
        functools.partial(_gather_rows_kernel, tile=tile, r=r, fanout=fanout, n_src=src.shape[0] // r),
        grid_spec=pltpu.PrefetchScalarGridSpec(
            num_scalar_prefetch=2,
            grid=(1,),
            in_specs=[pl.BlockSpec(memory_space=pl.ANY)],
            out_specs=pl.BlockSpec(memory_space=pl.ANY),
            scratch_shapes=[pltpu.VMEM((2, tile * r, LANES), src.dtype),
                            pltpu.SMEM((n_dst if fanout else 1,), I32),
                            pltpu.SemaphoreType.DMA((2,)), pltpu.SemaphoreType.DMA((2,))],
        ),
        out_shape=jax.ShapeDtypeStruct((n_dst * r, LANES), src.dtype),
        compiler_params=_params(("arbitrary",), 2 * tile * r * LANES * 4),
        name="gather_rows",
    )(idx, trips, src)


CAST_ROWS = 512


def _expert_weight_copies(w_hbm, stage, sems, layer, e):
    return [pltpu.make_async_copy(w.at[layer, e], s, sems.at[k]) for k, (w, s) in enumerate(zip(w_hbm, stage))]


def _expert_ffn_kernel(te_ref, first_ref, next_ref, nt_ref, x_ref, wg_hbm, wu_hbm, wd_hbm, o_ref,
                       wg_st, wu_st, wd_st, wg_bf, wu_bf, wd_bf, sems, *, layer, tm):
    i = pl.program_id(0)
    stage = (wg_st, wu_st, wd_st)
    copies = lambda e: _expert_weight_copies((wg_hbm, wu_hbm, wd_hbm), stage, sems, layer, e)

    @pl.when(i == 0)
    def _():
        for cp in copies(te_ref[0]):
            cp.start()

    @pl.when((i < nt_ref[0]) & (first_ref[i] == 1))
    def _():
        for cp in copies(te_ref[i]):
            cp.wait()
        for st, bf in zip(stage, (wg_bf, wu_bf, wd_bf)):
            rows = st.shape[0]
            for r0 in range(0, rows, CAST_ROWS):
                r1 = min(r0 + CAST_ROWS, rows)
                bf[r0:r1, :] = st[r0:r1, :].astype(BF16)

        @pl.when(next_ref[i] >= 0)
        def _():
            for cp in copies(next_ref[i]):
                cp.start()

    @pl.when(i < nt_ref[0])
    def _():
        xa, xb = _unpack_bf16_pair(_load_slabs(x_ref, tm))
        xa, xb = xa.astype(BF16), xb.astype(BF16)
        half = xa.shape[1]
        a = _dot(xa, wg_bf[:half, :]) + _dot(xb, wg_bf[half:, :])
        b = _dot(xa, wu_bf[:half, :]) + _dot(xb, wu_bf[half:, :])
        hid = (a * _sigmoid(a) * b).astype(BF16)
        ya = _dot(hid, wd_bf[:, :half])
        yb = _dot(hid, wd_bf[:, half:])
        _store_slabs(o_ref, _pack_bf16_pair(ya, yb))

    @pl.when(i >= nt_ref[0])
    def _():
        o_ref[...] = jnp.zeros(o_ref.shape, o_ref.dtype)


def _expert_ffn(xs, r, tile_expert, tile_first, tile_next, n_tiles, w_gate, w_up, w_down, layer, tm):
    p = xs.shape[0] // r
    d, f = w_gate.shape[2:]
    hbm = pl.BlockSpec(memory_space=pl.ANY)
    tile = pl.BlockSpec((tm * r, LANES), lambda i, *_: (i, 0))
    return pl.pallas_call(
        functools.partial(_expert_ffn_kernel, layer=layer, tm=tm),
        grid_spec=pltpu.PrefetchScalarGridSpec(
            num_scalar_prefetch=4,
            grid=(p // tm,),
            in_specs=[tile, hbm, hbm, hbm],
            out_specs=tile,
            scratch_shapes=[pltpu.VMEM((d, f), F32), pltpu.VMEM((d, f), F32), pltpu.VMEM((f, d), F32),
                            pltpu.VMEM((d, f), BF16), pltpu.VMEM((d, f), BF16), pltpu.VMEM((f, d), BF16),
                            pltpu.SemaphoreType.DMA((3,))],
        ),
        out_shape=jax.ShapeDtypeStruct((p * r, LANES), U32),
        compiler_params=_params(("arbitrary",), 3 * d * f * 6 + 4 * tm * r * LANES * 4 + 6 * tm * d * 4),
        name="expert_ffn",
    )(tile_expert, tile_first, tile_next, n_tiles, xs, w_gate, w_up, w_down)


def _moe_sparse(hp, r, logits, w_gate, w_up, w_down, layer, n_groups, per_group):
    t = hp.shape[0] // r
    n_exp = n_groups * per_group
    tm = 256 if t * TOP_K >= 256 * n_exp else 8
    meta, wts, cnt = _route(logits, n_groups, per_group)

    counts = cnt[0, n_groups:n_groups + n_exp]
    tiles_e = (counts + tm - 1) // tm
    tile_end = jnp.cumsum(tiles_e)
    base = (tile_end - tiles_e) * tm
    n_tiles = tile_end[-1]
    max_tiles = (t * TOP_K) // tm + n_exp
    all_tiles = jnp.arange(max_tiles, dtype=I32)
    tile_expert = jnp.searchsorted(tile_end, jnp.minimum(all_tiles, n_tiles - 1), side="right").astype(I32)
    slot = (base[meta[0:TOP_K]] + meta[TOP_K:2 * TOP_K]).astype(I32)

    prev_expert = jnp.concatenate([jnp.full((1,), -1, I32), tile_expert[:-1]])
    tile_first = ((all_tiles < n_tiles) & (tile_expert != prev_expert)).astype(I32)
    group_end = tile_end[tile_expert]
    tile_next = jnp.where(group_end < n_tiles, tile_expert[jnp.minimum(group_end, max_tiles - 1)], -1).astype(I32)

    n_tiles = n_tiles.reshape(1).astype(I32)
    xs = _gather_rows(hp, r, slot.T.reshape(-1), max_tiles * tm, fanout=TOP_K)
    ys = _expert_ffn(xs, r, tile_expert, tile_first, tile_next, n_tiles, w_gate, w_up, w_down, layer, tm)
    yk = _gather_rows(ys, r, slot.reshape(-1), TOP_K * t)
    return yk.reshape(TOP_K, t * r, LANES), wts


def kernel(x, c, w_cond, w_mod, b_mod, w_in, b_forget, conv_w, g_conv_out, g_attn_out, w_out, ln1_g, ln1_b, w_router_group, b_router_group, w_router_expert, b_router_expert, w_gate, w_up, w_down, ln2_g, ln2_b):
    bsz, seq, d = x.shape
    depth = w_mod.shape[0]
    d_conv = conv_w.shape[2]
    n_heads = b_forget.shape[1]
    d_att = n_heads * HEAD_DIM
    n_main = 3 * d_conv + 3 * d_att
    n_groups = w_router_group.shape[2]
    n_exp = w_router_expert.shape[2]
    alpha = (2 * depth) ** 0.25
    assert w_in.shape[2] == n_main + n_heads and n_groups + n_exp <= LANES
    assert d_conv % LANES == 0 and (d // 2) % LANES == 0

    mod = _modulation(c, w_cond, w_mod, b_mod).reshape(depth, bsz, 6, 1, d)
    part = lambda l, k: mod[l, :, k]
    pad = LANES - n_groups - n_exp
    w_router = jnp.pad(jnp.concatenate([w_router_group, w_router_expert], axis=2), ((0, 0), (0, 0), (0, pad)))
    b_router = jnp.pad(jnp.concatenate([b_router_group, b_router_expert], axis=1), ((0, 0), (0, pad)))
    w_in_t = jnp.swapaxes(w_in, 1, 2)
    q_cols = (3 * d_conv, 3 * d_conv + d_att)
    q_scale = HEAD_DIM ** -0.5 * LOG2E

    x2 = x.reshape(bsz * seq, d)
    h = _modulate(x2, part(0, 1), part(0, 0), seq)
    for l in range(depth):
        proj = _inproj(h, w_in_t, l, n_main, q_cols, q_scale)
        cum = _forget_cumsum(h, w_in_t[l, n_main:], b_forget[l], bsz, seq)
        y_conv = _gated_conv(proj, conv_w[l], g_conv_out[l], seq)
        o_att = _attention(proj, cum, bsz, seq, 3 * d_conv // LANES, n_heads)
        y = _outproj(y_conv, o_att, g_attn_out[l], w_out, l)
        x2, hp, logits = _ln_after_mixer(x2, y, part(l, 2), ln1_g[l], ln1_b[l], part(l, 4), part(l, 3),
                                         w_router[l], b_router[l].reshape(1, LANES), seq, alpha)
        yk, wts = _moe_sparse(hp, d // 2 // LANES, logits, w_gate, w_up, w_down, l, n_groups, n_exp // n_groups)
        if l + 1 < depth:
            x2, h = _ln_after_moe(x2, yk, wts, part(l, 5), ln2_g[l], ln2_b[l], seq, alpha,
                                  nxt=(part(l + 1, 1), part(l + 1, 0)))
        else:
            (x2,) = _ln_after_moe(x2, yk, wts, part(l, 5), ln2_g[l], ln2_b[l], seq, alpha)
    return x2.reshape(bsz, seq, d)
```

```python
import functools
import math

import jax
import jax.numpy as jnp
from jax import lax
from jax.experimental import pallas as pl
from jax.experimental.pallas import tpu as pltpu

F32 = jnp.float32
BF16 = jnp.bfloat16
U32 = jnp.uint32
I32 = jnp.int32

LANES = 128
SUBLANES = 8
VMEM_BYTES = 64 * 1024 * 1024
HEAD_DIM = 128
TOP_K = 2
LN_EPS = 1e-5
LOG2E = math.log2(math.e)
NEG_INF = float("-inf")
HI16 = 0xFFFF0000


def _tile(n, pref):
    t = min(n, pref)
    while n % t:
        t -= 1
    return t


def _params(semantics, vmem_bytes):
    limit = min(VMEM_BYTES - (4 << 20), vmem_bytes + (8 << 20))
    return pltpu.CompilerParams(dimension_semantics=semantics, vmem_limit_bytes=int(limit))


def _split2(a):
    hi = a.astype(BF16)
    lo = (a - hi.astype(F32)).astype(BF16)
    return hi, lo


def _dot(a, b):
    return jnp.dot(a, b, preferred_element_type=F32)


def _dot_nt(a, b):
    return lax.dot_general(a, b, (((1,), (1,)), ((), ())), preferred_element_type=F32)


def _dot_f32x3(a, b):
    ah, al = _split2(a)
    bh, bl = _split2(b)
    return _dot(ah, bh) + (_dot(al, bh) + _dot(ah, bl))


def _sigmoid(x):
    return 1.0 / (1.0 + jnp.exp(-x))


def _pack_bf16_pair(a, b):
    ua = lax.bitcast_convert_type(a.astype(BF16).astype(F32), U32)
    ub = lax.bitcast_convert_type(b.astype(BF16).astype(F32), U32)
    return ua | lax.shift_right_logical(ub, jnp.uint32(16))


def _unpack_bf16_pair(w):
    a = lax.bitcast_convert_type(w & jnp.uint32(HI16), F32)
    b = lax.bitcast_convert_type(lax.shift_left(w, jnp.uint32(16)), F32)
    return a, b


def _store_slabs(ref, val):
    rows, width = val.shape
    r = width // LANES
    for c in range(r):
        ref[pl.ds(c, rows, stride=r), :] = val[:, c * LANES:(c + 1) * LANES]


def _load_slabs(ref, rows):
    r = ref.shape[0] // rows
    return jnp.concatenate([ref[pl.ds(c, rows, stride=r), :] for c in range(r)], axis=1)


def _cond_embed_kernel(c_ref, w_ref, e_ref):
    c = c_ref[...]
    e_ref[...] = _dot_f32x3(c * _sigmoid(c), w_ref[...])


def _mod_kernel(e_ref, w_ref, b_ref, o_ref):
    o_ref[...] = _dot_f32x3(e_ref[...], w_ref[...]) + b_ref[...]


def _modulation(c, w_cond, w_mod, b_mod):
    bsz, d = c.shape
    depth, rank, n_mod = w_mod.shape
    rows = SUBLANES
    c_pad = jnp.pad(c, ((0, rows - bsz), (0, 0)))
    e = pl.pallas_call(
        _cond_embed_kernel,
        out_shape=jax.ShapeDtypeStruct((rows, rank), F32),
        compiler_params=_params(None, 8 * (rows * d + d * rank + rows * rank)),
        name="cond_embed",
    )(c_pad, w_cond)
    tn = _tile(n_mod, 4096)
    mod = pl.pallas_call(
        _mod_kernel,
        grid=(depth, n_mod // tn),
        in_specs=[
            pl.BlockSpec((rows, rank), lambda l, j: (0, 0)),
            pl.BlockSpec((None, rank, tn), lambda l, j: (l, 0, j)),
            pl.BlockSpec((None, 1, tn), lambda l, j: (l, 0, j)),
        ],
        out_specs=pl.BlockSpec((None, rows, tn), lambda l, j: (l, 0, j)),
        out_shape=jax.ShapeDtypeStruct((depth, rows, n_mod), F32),
        compiler_params=_params(("arbitrary", "arbitrary"), 8 * (rank * tn + rows * tn + tn + rows * rank)),
        name="adaln_mod",
    )(e, w_mod, b_mod.reshape(depth, 1, n_mod))
    return mod[:, :bsz]


def _modulate_kernel(x_ref, sc_ref, sh_ref, h_ref):
    h_ref[...] = (x_ref[...] * (1.0 + sc_ref[...]) + sh_ref[...]).astype(BF16)


def _modulate(x2, sc, sh, seq):
    t, d = x2.shape
    tm = _tile(seq, 512)
    per_b = seq // tm
    vec = pl.BlockSpec((None, 1, d), lambda i: (i // per_b, 0, 0))
    return pl.pallas_call(
        _modulate_kernel,
        grid=(t // tm,),
        in_specs=[pl.BlockSpec((tm, d), lambda i: (i, 0)), vec, vec],
        out_specs=pl.BlockSpec((tm, d), lambda i: (i, 0)),
        out_shape=jax.ShapeDtypeStruct((t, d), BF16),
        compiler_params=_params(("arbitrary",), tm * d * 12),
        name="modulate",
    )(x2, sc, sh)


def _inproj_kernel(h_ref, w_ref, o_ref, wbf_ref, *, q_tiles, q_scale):
    @pl.when(pl.program_id(1) == 0)
    def _():
        j = pl.program_id(0)
        scale = jnp.where((j >= q_tiles[0]) & (j < q_tiles[1]), q_scale, 1.0).astype(F32)
        wbf_ref[...] = (w_ref[...] * scale).astype(BF16)

    acc = _dot_nt(h_ref[...], wbf_ref[...])
    for cb in range(o_ref.shape[0]):
        o_ref[cb] = acc[:, cb * LANES:(cb + 1) * LANES].astype(BF16)


def _inproj(h, w_in_t, layer, n_main, q_cols, q_scale):
    t, d = h.shape
    tm = _tile(t, 1024)
    tn = _tile(math.gcd(q_cols[0], q_cols[1] - q_cols[0]), 512)
    q_tiles = (q_cols[0] // tn, q_cols[1] // tn)
    return pl.pallas_call(
        functools.partial(_inproj_kernel, q_tiles=q_tiles, q_scale=q_scale),
        grid=(n_main // tn, t // tm),
        in_specs=[
            pl.BlockSpec((tm, d), lambda j, i: (i, 0)),
            pl.BlockSpec((None, tn, d), lambda j, i: (layer, j, 0)),
        ],
        out_specs=pl.BlockSpec((tn // LANES, tm, LANES), lambda j, i: (j, i, 0)),
        out_shape=jax.ShapeDtypeStruct((n_main // LANES, t, LANES), BF16),
        scratch_shapes=[pltpu.VMEM((tn, d), BF16)],
        compiler_params=_params(("arbitrary", "arbitrary"), 2 * tm * d * 2 + d * tn * 10 + 3 * tm * tn * 4),
        name="in_proj",
    )(h, w_in_t)


def _forget_kernel(h_ref, w_ref, b_ref, o_ref, carry_ref):
    @pl.when(pl.program_id(1) == 0)
    def _():
        carry_ref[...] = jnp.zeros_like(carry_ref)

    h = h_ref[...]
    w_hi, w_lo = _split2(w_ref[...])
    f = _dot_nt(w_hi, h) + _dot_nt(w_lo, h) + b_ref[...]
    log_f = jnp.minimum(f, 0.0) - jnp.log(1.0 + jnp.exp(-jnp.abs(f)))
    nh, bs = log_f.shape
    row = lax.broadcasted_iota(I32, (bs, bs), 0)
    col = lax.broadcasted_iota(I32, (bs, bs), 1)
    triu = jnp.where(row <= col, 1.0, 0.0).astype(BF16)
    p0 = log_f.astype(BF16)
    r1 = log_f - p0.astype(F32)
    p1 = r1.astype(BF16)
    p2 = (r1 - p1.astype(F32)).astype(BF16)
    cs = _dot(p0, triu) + (_dot(p1, triu) + _dot(p2, triu)) + carry_ref[...]
    carry_ref[...] = cs[:, bs - 1:bs]
    o_ref[...] = cs


def _forget_cumsum(h, w_f_t, b_f, bsz, seq):
    t, d = h.shape
    nh = w_f_t.shape[0]
    bs = _tile(seq, 512)
    per_b = seq // bs
    return pl.pallas_call(
        _forget_kernel,
        grid=(bsz, per_b),
        in_specs=[
            pl.BlockSpec((bs, d), lambda b, i: (b * per_b + i, 0)),
            pl.BlockSpec((nh, d), lambda b, i: (0, 0)),
            pl.BlockSpec((nh, 1), lambda b, i: (0, 0)),
        ],
        out_specs=pl.BlockSpec((None, nh, bs), lambda b, i: (b, 0, i)),
        out_shape=jax.ShapeDtypeStruct((bsz, nh, seq), F32),
        scratch_shapes=[pltpu.VMEM((nh, 1), F32)],
        compiler_params=_params(("arbitrary", "arbitrary"), 2 * bs * d * 2 + nh * d * 16 + bs * bs * 8),
        name="forget_cumsum",
    )(h, w_f_t, b_f.reshape(nh, 1))


def _conv_kernel(bg_ref, cg_ref, hc_ref, w_ref, g_ref, o_ref, ubuf_ref, *, per_b, d_conv):
    i = pl.program_id(0)
    ts = bg_ref.shape[1]
    halo = SUBLANES

    @pl.when(i % per_b == 0)
    def _():
        ubuf_ref[:, 0:halo, :] = jnp.zeros((ubuf_ref.shape[0], halo, LANES), F32)

    @pl.when(i % per_b != 0)
    def _():
        ubuf_ref[:, 0:halo, :] = ubuf_ref[:, ts:ts + halo, :]

    u = cg_ref[...].astype(F32) * hc_ref[...].astype(F32)
    ubuf_ref[:, halo:halo + ts, :] = u
    u1 = ubuf_ref[:, halo - 1:halo - 1 + ts, :]
    u2 = ubuf_ref[:, halo - 2:halo - 2 + ts, :]
    conv = u2 * w_ref[0] + u1 * w_ref[1] + u * w_ref[2]
    y = bg_ref[...].astype(F32) * conv
    ssq = jnp.sum(jnp.sum(y * y, axis=0), axis=-1, keepdims=True)
    r = lax.rsqrt(ssq * (1.0 / d_conv) + LN_EPS)
    o_ref[...] = (y * r[None] * g_ref[...]).astype(BF16)


def _gated_conv(proj, conv_w, g_conv, seq):
    _, t, _ = proj.shape
    width, d_conv = conv_w.shape
    assert width == 3, "causal depthwise convolution of width 3"
    nc = d_conv // LANES
    ts = _tile(seq, 512)
    per_b = seq // ts
    slab = lambda k: pl.BlockSpec((nc, ts, LANES), lambda i: (k, i, 0))
    return pl.pallas_call(
        functools.partial(_conv_kernel, per_b=per_b, d_conv=d_conv),
        grid=(t // ts,),
        in_specs=[
            slab(0), slab(1), slab(2),
            pl.BlockSpec((width, nc, 1, LANES), lambda i: (0, 0, 0, 0)),
            pl.BlockSpec((nc, 1, LANES), lambda i: (0, 0, 0)),
        ],
        out_specs=pl.BlockSpec((nc, ts, LANES), lambda i: (0, i, 0)),
        out_shape=jax.ShapeDtypeStruct((nc, t, LANES), BF16),
        scratch_shapes=[pltpu.VMEM((nc, ts + 2 * SUBLANES, LANES), F32)],
        compiler_params=_params(("arbitrary",), nc * ts * LANES * (2 * 4 * 2 + 4 * 8)),
        name="gated_conv",
    )(proj, proj, proj, conv_w.reshape(width, nc, 1, LANES), g_conv.reshape(nc, 1, LANES))


def _attn_kernel(q_ref, k_ref, v_ref, ck_ref, o_ref, m_ref, l_ref, acc_ref, *, blk):
    qi = pl.program_id(2)
    n_heads = q_ref.shape[0]
    m_ref[...] = jnp.full(m_ref.shape, NEG_INF, F32)
    l_ref[...] = jnp.zeros(l_ref.shape, F32)
    acc_ref[...] = jnp.zeros(acc_ref.shape, F32)
    ones = jnp.ones((blk, LANES), BF16)

    def step(kb, masked):
        ks = pl.multiple_of(kb * blk, blk)
        for g in range(n_heads):
            k = k_ref[g, pl.ds(ks, blk), :]
            v = v_ref[g, pl.ds(ks, blk), :]
            t = _dot_nt(q_ref[g], k) - ck_ref[g, kb] * LOG2E
            if masked:
                row = lax.broadcasted_iota(I32, t.shape, 0)
                col = lax.broadcasted_iota(I32, t.shape, 1)
                t = jnp.where(col <= row, t, NEG_INF)
            m = m_ref[g]
            m_new = jnp.maximum(m, jnp.broadcast_to(jnp.max(t, axis=1, keepdims=True), m.shape))
            p = jnp.exp2(t - jnp.tile(m_new, (1, blk // LANES)))
            alpha = jnp.exp2(m - m_new)
            pv = _dot(p.astype(BF16), jnp.concatenate([v, ones], axis=1))
            l_ref[g] = alpha * l_ref[g] + pv[:, HEAD_DIM:]
            acc_ref[g] = alpha * acc_ref[g] + pv[:, :HEAD_DIM]
            m_ref[g] = m_new

    def body(kb, carry):
        step(kb, False)
        return carry

    lax.fori_loop(0, qi, body, 0)
    step(qi, True)
    for g in range(n_heads):
        o_ref[g] = (acc_ref[g] / l_ref[g]).astype(BF16)


def _attention(proj, cum, bsz, seq, q_slab, n_heads):
    n_slabs = proj.shape[0]
    blk = _tile(seq, 256)
    nq = seq // blk
    grp = _tile(math.gcd(n_heads, q_slab), 8)
    proj4 = proj.reshape(n_slabs, bsz, seq, LANES)
    ck = cum.reshape(bsz, n_heads, nq, 1, blk)
    kv = lambda off: pl.BlockSpec((grp, None, seq, LANES), lambda b, h, i: (off // grp + h, b, 0, 0))
    out = pl.pallas_call(
        functools.partial(_attn_kernel, blk=blk),
        grid=(bsz, n_heads // grp, nq),
        in_specs=[
            pl.BlockSpec((grp, None, blk, LANES), lambda b, h, i: (q_slab // grp + h, b, i, 0)),
            kv(q_slab + n_heads), kv(q_slab + 2 * n_heads),
            pl.BlockSpec((None, grp, nq, 1, blk), lambda b, h, i: (b, h, 0, 0, 0)),
        ],
        out_specs=pl.BlockSpec((grp, None, blk, LANES), lambda b, h, i: (h, b, i, 0)),
        out_shape=jax.ShapeDtypeStruct((n_heads, bsz, seq, LANES), BF16),
        scratch_shapes=[
            pltpu.VMEM((grp, blk, LANES), F32),
            pltpu.VMEM((grp, blk, LANES), F32),
            pltpu.VMEM((grp, blk, HEAD_DIM), F32),
        ],
        compiler_params=_params(("arbitrary", "arbitrary", "arbitrary"),
                                grp * (4 * seq * LANES * 2 + 2 * seq * 4 * SUBLANES + 4 * blk * LANES * 2
                                       + 2 * blk * LANES * 4 + blk * HEAD_DIM * 4 + 8 * blk * blk * 4)),
        name="fox_attention",
    )(proj4, proj4, proj4, ck)
    return out.reshape(n_heads, bsz * seq, LANES)


def _outproj_kernel(yc_ref, oa_ref, ga_ref, w_ref, o_ref, ybuf_ref, *, d_att):
    nc, na = yc_ref.shape[0], oa_ref.shape[0]

    @pl.when(pl.program_id(1) == 0)
    def _():
        for cb in range(nc):
            ybuf_ref[:, cb * LANES:(cb + 1) * LANES] = yc_ref[cb]
        oa = oa_ref[...].astype(F32)
        ssq = jnp.sum(jnp.sum(oa * oa, axis=0), axis=-1, keepdims=True)
        r = lax.rsqrt(ssq * (1.0 / d_att) + LN_EPS)
        for cb in range(na):
            ybuf_ref[:, (nc + cb) * LANES:(nc + cb + 1) * LANES] = (oa[cb] * r * ga_ref[cb]).astype(BF16)

    o_ref[...] = _dot(ybuf_ref[...], w_ref[...].astype(BF16)).astype(BF16)


def _outproj(y_conv, o_att, g_attn, w_out, layer):
    nc, t, _ = y_conv.shape
    na = o_att.shape[0]
    d_mix, d = w_out.shape[1:]
    tm = _tile(t, 1024)
    tn = _tile(d, 512)
    return pl.pallas_call(
        functools.partial(_outproj_kernel, d_att=na * LANES),
        grid=(t // tm, d // tn),
        in_specs=[
            pl.BlockSpec((nc, tm, LANES), lambda i, j: (0, i, 0)),
            pl.BlockSpec((na, tm, LANES), lambda i, j: (0, i, 0)),
            pl.BlockSpec((na, 1, LANES), lambda i, j: (0, 0, 0)),
            pl.BlockSpec((None, d_mix, tn), lambda i, j: (layer, 0, j)),
        ],
        out_specs=pl.BlockSpec((tm, tn), lambda i, j: (i, j)),
        out_shape=jax.ShapeDtypeStruct((t, d), BF16),
        scratch_shapes=[pltpu.VMEM((tm, d_mix), BF16)],
        compiler_params=_params(("arbitrary", "arbitrary"),
                                2 * tm * d_mix * 2 + tm * d_mix * 2 + d_mix * tn * 10 + 3 * tm * tn * 4),
        name="out_proj",
    )(y_conv, o_att, g_attn.reshape(na, 1, LANES), w_out)


def _ln_finish(z, g_ref, b_ref):
    mu = jnp.mean(z, axis=-1, keepdims=True)
    zc = z - mu
    var = jnp.mean(zc * zc, axis=-1, keepdims=True)
    return zc * lax.rsqrt(var + LN_EPS) * g_ref[...] + b_ref[...]


def _ln_mixer_kernel(x_ref, y_ref, gt_ref, g_ref, b_ref, sc_ref, sh_ref, wr_ref, br_ref,
                     xo_ref, hp_ref, lg_ref, *, alpha):
    xn = _ln_finish(alpha * x_ref[...] + (1.0 + gt_ref[...]) * y_ref[...].astype(F32), g_ref, b_ref)
    h = xn * (1.0 + sc_ref[...]) + sh_ref[...]
    half = h.shape[1] // 2
    xo_ref[...] = xn
    _store_slabs(hp_ref, _pack_bf16_pair(h[:, :half], h[:, half:]))
    lg_ref[...] = _dot_f32x3(h, wr_ref[...]) + br_ref[...]


def _ln_moe_kernel(*refs, alpha, with_h):
    x_ref, y0_ref, y1_ref, wt_ref, gt_ref, g_ref, b_ref = refs[:7]
    wt = wt_ref[...]
    w0, w1 = wt[:, 0:1], wt[:, 1:2]
    tm = x_ref.shape[0]
    a0, b0 = _unpack_bf16_pair(_load_slabs(y0_ref, tm))
    a1, b1 = _unpack_bf16_pair(_load_slabs(y1_ref, tm))
    y = jnp.concatenate([w0 * a0 + w1 * a1, w0 * b0 + w1 * b1], axis=1)
    xn = _ln_finish(alpha * x_ref[...] + (1.0 + gt_ref[...]) * y, g_ref, b_ref)
    if with_h:
        sc_ref, sh_ref, xo_ref, ho_ref = refs[7:]
        ho_ref[...] = (xn * (1.0 + sc_ref[...]) + sh_ref[...]).astype(BF16)
    else:
        (xo_ref,) = refs[7:]
    xo_ref[...] = xn


def _ln_specs(t, d, seq, tm):
    per_b = seq // tm
    tok = pl.BlockSpec((tm, d), lambda i: (i, 0))
    vec = pl.BlockSpec((None, 1, d), lambda i: (i // per_b, 0, 0))
    row = pl.BlockSpec((1, d), lambda i: (0, 0))
    return tok, vec, row


def _ln_after_mixer(x2, y, gt, g, b, sc, sh, w_router, b_router, seq, alpha):
    t, d = x2.shape
    tm = _tile(seq, 256)
    tok, vec, row = _ln_specs(t, d, seq, tm)
    r = d // 2 // LANES
    return pl.pallas_call(
        functools.partial(_ln_mixer_kernel, alpha=alpha),
        grid=(t // tm,),
        in_specs=[tok, tok, vec, row, row, vec, vec,
                  pl.BlockSpec((d, LANES), lambda i: (0, 0)), pl.BlockSpec((1, LANES), lambda i: (0, 0))],
        out_specs=[tok, pl.BlockSpec((tm * r, LANES), lambda i: (i, 0)), pl.BlockSpec((tm, LANES), lambda i: (i, 0))],
        out_shape=[jax.ShapeDtypeStruct((t, d), F32), jax.ShapeDtypeStruct((t * r, LANES), U32),
                   jax.ShapeDtypeStruct((t, LANES), F32)],
        compiler_params=_params(("arbitrary",), tm * d * 44 + d * LANES * 16),
        name="ln_mixer",
    )(x2, y, gt, g.reshape(1, d), b.reshape(1, d), sc, sh, w_router, b_router)


def _ln_after_moe(x2, yk, wts, gt, g, b, seq, alpha, nxt=None):
    t, d = x2.shape
    tm = _tile(seq, 256)
    tok, vec, row = _ln_specs(t, d, seq, tm)
    r = yk.shape[1] // t
    yk_spec = lambda k: pl.BlockSpec((None, tm * r, LANES), lambda i: (k, i, 0))
    ins = [x2, yk, yk, wts, gt, g.reshape(1, d), b.reshape(1, d)]
    in_specs = [tok, yk_spec(0), yk_spec(1), pl.BlockSpec((tm, LANES), lambda i: (i, 0)), vec, row, row]
    outs, out_specs = [jax.ShapeDtypeStruct((t, d), F32)], [tok]
    if nxt is not None:
        ins += list(nxt)
        in_specs += [vec, vec]
        outs.append(jax.ShapeDtypeStruct((t, d), BF16))
        out_specs.append(tok)
    return pl.pallas_call(
        functools.partial(_ln_moe_kernel, alpha=alpha, with_h=nxt is not None),
        grid=(t // tm,),
        in_specs=in_specs,
        out_specs=out_specs,
        out_shape=outs,
        compiler_params=_params(("arbitrary",), tm * d * 44),
        name="ln_moe",
    )(*ins)


def _route_kernel(lg_ref, meta_ref, wts_ref, cnt_ref, carry_ref, *, n_groups, per_group):
    @pl.when(pl.program_id(0) == 0)
    def _():
        carry_ref[...] = jnp.zeros_like(carry_ref)

    lg = lg_ref[...]
    tb = lg.shape[0]
    lane = lax.broadcasted_iota(I32, lg.shape, 1)
    n_exp = n_groups * per_group

    def first_argmax(vals, vmax):
        return jnp.min(jnp.where(vals == vmax, lane, LANES), axis=1, keepdims=True)

    gl = jnp.where(lane < n_groups, lg, NEG_INF)
    g_max = jnp.max(gl, axis=1, keepdims=True)
    g_val = 1.0 / jnp.sum(jnp.exp(gl - g_max), axis=1, keepdims=True)
    g_idx = first_argmax(gl, g_max)
    lo = n_groups + g_idx * per_group
    el = jnp.where((lane >= lo) & (lane < lo + per_group) & (lane < n_groups + n_exp), lg, NEG_INF)
    e1 = jnp.max(el, axis=1, keepdims=True)
    i1 = first_argmax(el, e1)
    el2 = jnp.where(lane == i1, NEG_INF, el)
    e2 = jnp.max(el2, axis=1, keepdims=True)
    i2 = first_argmax(el2, e2)
    d = jnp.exp(e2 - e1)
    w1 = g_val / (1.0 + d)
    w2 = g_val * d / (1.0 + d)
    wts_ref[...] = jnp.where(lane == 0, w1, jnp.where(lane == 1, w2, 0.0))

    sel = jnp.where((lane == i1) | (lane == i2), 1.0, 0.0)
    row = lax.broadcasted_iota(I32, (tb, tb), 0)
    col = lax.broadcasted_iota(I32, (tb, tb), 1)
    strict_lower = jnp.where(col < row, 1.0, 0.0).astype(BF16)
    rank = _dot(strict_lower, sel.astype(BF16)) + carry_ref[...]
    pos1 = jnp.sum(jnp.where(lane == i1, rank, 0.0), axis=1, keepdims=True).astype(I32)
    pos2 = jnp.sum(jnp.where(lane == i2, rank, 0.0), axis=1, keepdims=True).astype(I32)
    total = rank[tb - 1:tb, :] + sel[tb - 1:tb, :]
    carry_ref[...] = total
    cnt_ref[...] = total.astype(I32)
    meta = jnp.where(lane == 0, i1 - n_groups, jnp.where(lane == 1, i2 - n_groups,
                     jnp.where(lane == 2, pos1, jnp.where(lane == 3, pos2, 0))))
    meta_ref[...] = meta.T[:SUBLANES, :]


def _route(logits, n_groups, per_group):
    t = logits.shape[0]
    tb = _tile(t, 512)
    blk = pl.BlockSpec((tb, LANES), lambda i: (i, 0))
    one = pl.BlockSpec((1, LANES), lambda i: (0, 0))
    return pl.pallas_call(
        functools.partial(_route_kernel, n_groups=n_groups, per_group=per_group),
        grid=(t // tb,),
        in_specs=[blk],
        out_specs=[pl.BlockSpec((SUBLANES, tb), lambda i: (0, i)), blk, one],
        out_shape=[jax.ShapeDtypeStruct((SUBLANES, t), I32), jax.ShapeDtypeStruct((t, LANES), F32),
                   jax.ShapeDtypeStruct((1, LANES), I32)],
        scratch_shapes=[pltpu.VMEM((1, LANES), F32)],
        compiler_params=_params(("arbitrary",), tb * LANES * 4 * 24 + tb * tb * 8),
        name="route",
    )(logits)


SCALAR_UNROLL = 8


def _gather_rows_kernel(idx_ref, trips_ref, src_ref, dst_ref, vbuf_ref, row_ref, gsem, osem,
                        *, tile, r, fanout, n_src):
    n_tiles = trips_ref[2]
    if fanout:
        def clear(i, c):
            for u in range(SCALAR_UNROLL):
                row_ref[i * SCALAR_UNROLL + u] = lax.rem(i * SCALAR_UNROLL + u, n_src)
            return c

        def put(i, c):
            base = i * SCALAR_UNROLL
            dst = [idx_ref[base + u] for u in range(SCALAR_UNROLL)]
            for u in range(SCALAR_UNROLL):
                row_ref[dst[u]] = lax.div(base + u, fanout)
            return c

        lax.fori_loop(0, lax.div(trips_ref[0], SCALAR_UNROLL), clear, 0)
        lax.fori_loop(0, lax.div(trips_ref[1], SCALAR_UNROLL), put, 0)
        rows = row_ref
    else:
        rows = idx_ref

    def gather_start(ti, slot):
        def group(i, c):
            base = i * SCALAR_UNROLL
            srcs = [rows[ti * tile + base + u] for u in range(SCALAR_UNROLL)]
            for u in range(SCALAR_UNROLL):
                s = pl.multiple_of(srcs[u] * r, r)
                d = pl.multiple_of((base + u) * r, r)
                pltpu.make_async_copy(src_ref.at[pl.ds(s, r)], vbuf_ref.at[slot, pl.ds(d, r)],
                                      gsem.at[slot]).start(priority=u % 2)
            return c

        lax.fori_loop(0, tile // SCALAR_UNROLL, group, 0)

    def gather_wait(slot):
        pltpu.make_async_copy(src_ref.at[pl.ds(0, tile * r)], vbuf_ref.at[slot], gsem.at[slot]).wait()

    def tile_out(ti, slot):
        start = pl.multiple_of(ti * (tile * r), tile * r)
        return pltpu.make_async_copy(vbuf_ref.at[slot], dst_ref.at[pl.ds(start, tile * r)], osem.at[slot])

    gather_start(0, 0)

    def step(ti, c):
        slot = ti % 2

        @pl.when(ti >= 1)
        def _():
            tile_out(ti - 1, 1 - slot).wait()

        @pl.when(ti + 1 < n_tiles)
        def _():
            gather_start(ti + 1, 1 - slot)

        gather_wait(slot)
        tile_out(ti, slot).start()
        return c

    lax.fori_loop(0, n_tiles, step, 0)
    tile_out(n_tiles - 1, (n_tiles - 1) % 2).wait()


GATHER_TILE = 256


def _gather_rows(src, r, idx, n_dst, fanout=0):
    tile = _tile(n_dst, GATHER_TILE)
    assert src.shape[0] >= tile * r and tile % SCALAR_UNROLL == 0 and idx.shape[0] % SCALAR_UNROLL == 0
    trips = jnp.array([n_dst, idx.shape[0], n_dst // tile], I32)
    return pl.pallas_call(
        functools.partial(_gather_rows_kernel, tile=tile, r=r, fanout=fanout, n_src=src.shape[0] // r),
        grid_spec=pltpu.PrefetchScalarGridSpec(
            num_scalar_prefetch=2,
            grid=(1,),
            in_specs=[pl.BlockSpec(memory_space=pl.ANY)],
            out_specs=pl.BlockSpec(memory_space=pl.ANY),
            scratch_shapes=[pltpu.VMEM((2, tile * r, LANES), src.dtype),
                            pltpu.SMEM((n_dst if fanout else 1,), I32),
                            pltpu.SemaphoreType.DMA((2,)), pltpu.SemaphoreType.DMA((2,))],
        ),
        out_shape=jax.ShapeDtypeStruct((n_dst * r, LANES), src.dtype),
        compiler_params=_params(("arbitrary",), 2 * tile * r * LANES * 4),
        name="gather_rows",
    )(idx, trips, src)


CAST_ROWS = 512


def _expert_weight_copies(w_hbm, stage, sems, layer, e):
    return [pltpu.make_async_copy(w.at[layer, e], s, sems.at[k]) for k, (w, s) in enumerate(zip(w_hbm, stage))]


def _expert_ffn_kernel(te_ref, first_ref, next_ref, nt_ref, x_ref, wg_hbm, wu_hbm, wd_hbm, o_ref,
                       wg_st, wu_st, wd_st, wg_bf, wu_bf, wd_bf, sems, *, layer, tm):
    i = pl.program_id(0)
    stage = (wg_st, wu_st, wd_st)
    copies = lambda e: _expert_weight_copies((wg_hbm, wu_hbm, wd_hbm), stage, sems, layer, e)

    @pl.when(i == 0)
    def _():
        for cp in copies(te_ref[0]):
            cp.start()

    @pl.when((i < nt_ref[0]) & (first_ref[i] == 1))
    def _():
        for cp in copies(te_ref[i]):
            cp.wait()
        for st, bf in zip(stage, (wg_bf, wu_bf, wd_bf)):
            rows = st.shape[0]
            for r0 in range(0, rows, CAST_ROWS):
                r1 = min(r0 + CAST_ROWS, rows)
                bf[r0:r1, :] = st[r0:r1, :].astype(BF16)

        @pl.when(next_ref[i] >= 0)
        def _():
            for cp in copies(next_ref[i]):
                cp.start()

    @pl.when(i < nt_ref[0])
    def _():
        xa, xb = _unpack_bf16_pair(_load_slabs(x_ref, tm))
        xa, xb = xa.astype(BF16), xb.astype(BF16)
        half = xa.shape[1]
        a = _dot(xa, wg_bf[:half, :]) + _dot(xb, wg_bf[half:, :])
        b = _dot(xa, wu_bf[:half, :]) + _dot(xb, wu_bf[half:, :])
        hid = (a * _sigmoid(a) * b).astype(BF16)
        ya = _dot(hid, wd_bf[:, :half])
        yb = _dot(hid, wd_bf[:, half:])
        _store_slabs(o_ref, _pack_bf16_pair(ya, yb))

    @pl.when(i >= nt_ref[0])
    def _():
        o_ref[...] = jnp.zeros(o_ref.shape, o_ref.dtype)


def _expert_ffn(xs, r, tile_expert, tile_first, tile_next, n_tiles, w_gate, w_up, w_down, layer, tm):
    p = xs.shape[0] // r
    d, f = w_gate.shape[2:]
    hbm = pl.BlockSpec(memory_space=pl.ANY)
    tile = pl.BlockSpec((tm * r, LANES), lambda i, *_: (i, 0))
    return pl.pallas_call(
        functools.partial(_expert_ffn_kernel, layer=layer, tm=tm),
        grid_spec=pltpu.PrefetchScalarGridSpec(
            num_scalar_prefetch=4,
            grid=(p // tm,),
            in_specs=[tile, hbm, hbm, hbm],
            out_specs=tile,
            scratch_shapes=[pltpu.VMEM((d, f), F32), pltpu.VMEM((d, f), F32), pltpu.VMEM((f, d), F32),
                            pltpu.VMEM((d, f), BF16), pltpu.VMEM((d, f), BF16), pltpu.VMEM((f, d), BF16),
                            pltpu.SemaphoreType.DMA((3,))],
        ),
        out_shape=jax.ShapeDtypeStruct((p * r, LANES), U32),
        compiler_params=_params(("arbitrary",), 3 * d * f * 6 + 4 * tm * r * LANES * 4 + 6 * tm * d * 4),
        name="expert_ffn",
    )(tile_expert, tile_first, tile_next, n_tiles, xs, w_gate, w_up, w_down)


def _moe_sparse(hp, r, logits, w_gate, w_up, w_down, layer, n_groups, per_group):
    t = hp.shape[0] // r
    n_exp = n_groups * per_group
    tm = 256 if t * TOP_K >= 256 * n_exp else 8
    meta, wts, cnt = _route(logits, n_groups, per_group)

    counts = cnt[0, n_groups:n_groups + n_exp]
    tiles_e = (counts + tm - 1) // tm
    tile_end = jnp.cumsum(tiles_e)
    base = (tile_end - tiles_e) * tm
    n_tiles = tile_end[-1]
    max_tiles = (t * TOP_K) // tm + n_exp
    all_tiles = jnp.arange(max_tiles, dtype=I32)
    tile_expert = jnp.searchsorted(tile_end, jnp.minimum(all_tiles, n_tiles - 1), side="right").astype(I32)
    one_hot = meta[0:TOP_K, :, None] == jnp.arange(n_exp, dtype=I32)
    slot = (jnp.sum(jnp.where(one_hot, base, 0), axis=-1) + meta[TOP_K:2 * TOP_K]).astype(I32)

    prev_expert = jnp.concatenate([jnp.full((1,), -1, I32), tile_expert[:-1]])
    tile_first = ((all_tiles < n_tiles) & (tile_expert != prev_expert)).astype(I32)
    group_end = tile_end[tile_expert]
    tile_next = jnp.where(group_end < n_tiles, tile_expert[jnp.minimum(group_end, max_tiles - 1)], -1).astype(I32)

    n_tiles = n_tiles.reshape(1).astype(I32)
    xs = _gather_rows(hp, r, slot.T.reshape(-1), max_tiles * tm, fanout=TOP_K)
    ys = _expert_ffn(xs, r, tile_expert, tile_first, tile_next, n_tiles, w_gate, w_up, w_down, layer, tm)
    yk = _gather_rows(ys, r, slot.reshape(-1), TOP_K * t)
    return yk.reshape(TOP_K, t * r, LANES), wts


def kernel(x, c, w_cond, w_mod, b_mod, w_in, b_forget, conv_w, g_conv_out, g_attn_out, w_out, ln1_g, ln1_b, w_router_group, b_router_group, w_router_expert, b_router_expert, w_gate, w_up, w_down, ln2_g, ln2_b):
    bsz, seq, d = x.shape
    depth = w_mod.shape[0]
    d_conv = conv_w.shape[2]
    n_heads = b_forget.shape[1]
    d_att = n_heads * HEAD_DIM
    n_main = 3 * d_conv + 3 * d_att
    n_groups = w_router_group.shape[2]
    n_exp = w_router_expert.shape[2]
    alpha = (2 * depth) ** 0.25
    assert w_in.shape[2] == n_main + n_heads and n_groups + n_exp <= LANES
    assert d_conv % LANES == 0 and (d // 2) % LANES == 0

    mod = _modulation(c, w_cond, w_mod, b_mod).reshape(depth, bsz, 6, 1, d)
    part = lambda l, k: mod[l, :, k]
    pad = LANES - n_groups - n_exp
    w_router = jnp.pad(jnp.concatenate([w_router_group, w_router_expert], axis=2), ((0, 0), (0, 0), (0, pad)))
    b_router = jnp.pad(jnp.concatenate([b_router_group, b_router_expert], axis=1), ((0, 0), (0, pad)))
    w_in_t = jnp.swapaxes(w_in, 1, 2)
    q_cols = (3 * d_conv, 3 * d_conv + d_att)
    q_scale = HEAD_DIM ** -0.5 * LOG2E

    x2 = x.reshape(bsz * seq, d)
    h = _modulate(x2, part(0, 1), part(0, 0), seq)
    for l in range(depth):
        proj = _inproj(h, w_in_t, l, n_main, q_cols, q_scale)
        cum = _forget_cumsum(h, w_in_t[l, n_main:], b_forget[l], bsz, seq)
        y_conv = _gated_conv(proj, conv_w[l], g_conv_out[l], seq)
        o_att = _attention(proj, cum, bsz, seq, 3 * d_conv // LANES, n_heads)
        y = _outproj(y_conv, o_att, g_attn_out[l], w_out, l)
        x2, hp, logits = _ln_after_mixer(x2, y, part(l, 2), ln1_g[l], ln1_b[l], part(l, 4), part(l, 3),
                                         w_router[l], b_router[l].reshape(1, LANES), seq, alpha)
        yk, wts = _moe_sparse(hp, d // 2 // LANES, logits, w_gate, w_up, w_down, l, n_groups, n_exp // n_groups)
        if l + 1 < depth:
            x2, h = _ln_after_moe(x2, yk, wts, part(l, 5), ln2_g[l], ln2_b[l], seq, alpha,
                                  nxt=(part(l + 1, 1), part(l + 1, 0)))
        else:
            (x2,) = _ln_after_moe(x2, yk, wts, part(l, 5), ln2_g[l], ln2_b[l], seq, alpha)
    return x2.reshape(bsz, seq, d)
```
